```python
import math
import jax
import jax.numpy as jnp
from jax import lax
import numpy as np

D_MODEL = 1024
BATCH = 32
SEQ = 2048
DEPTH = 4

CTX_LEN = 256
GRID_W = 64
EPS = 1e-6
NEG_INF = -1e30
F32 = jnp.float32

MIX_WIDTH = D_MODEL
RET_HEADS = 4
RET_V = MIX_WIDTH // 4
RET_DV = RET_V // RET_HEADS
RET_DK = RET_DV // 2
RET_QK = RET_HEADS * RET_DK
RET_CHUNK = 128
RET_ROT_BASE = 10000.0
HY_CH = MIX_WIDTH // 4
HY_EMB = 33
HY_BANDS = (HY_EMB - 1) // 2
HY_ORDER = 64
HY_INNER = 2
HY_FAST_DECAY = 0.3
HY_SLOW_DECAY = 1.5
HY_TARGET = 1e-2
ATT_HEADS = 8
ATT_KV_HEADS = 2
ATT_GROUP = ATT_HEADS // ATT_KV_HEADS
ATT_Q = MIX_WIDTH - RET_V - HY_CH
ATT_HD = ATT_Q // ATT_HEADS
ATT_KV = ATT_KV_HEADS * ATT_HD
WINDOW = 128
ATT_BLOCK = 128
ATT_HALO = -(-WINDOW // ATT_BLOCK)
ROPE_BASE = 10000.0
D_FF = 4 * D_MODEL

SPLIT_SIZES = (RET_QK, RET_QK, RET_V, RET_V, 3 * HY_CH, ATT_Q, ATT_KV, ATT_KV)
D_IN = 2 * RET_QK + 2 * RET_V + 3 * HY_CH + ATT_Q + 2 * ATT_KV

kernel_name = 'hybrid_retention_hyena_swa_prefix_dit'


def rms_norm(x, g=None):
    xf = x.astype(F32)
    y = xf * lax.rsqrt(jnp.mean(xf * xf, axis=-1, keepdims=True) + EPS)
    if g is not None:
        y = y * g.astype(F32)
    return y.astype(x.dtype)


def modulate(h, shift, scale):
    return h * (1.0 + scale) + shift


def rotate_half(x):
    x1, x2 = jnp.split(x, 2, axis=-1)
    return jnp.concatenate([-x2, x1], axis=-1)


def rotary_table(pos_f, inv_freq):
    ang = pos_f[:, None] * inv_freq[None, :]
    ang = jnp.concatenate([ang, ang], axis=-1)
    return jnp.cos(ang), jnp.sin(ang)


def apply_rotary(x, cos, sin):
    cos = cos[None, :, None, :].astype(x.dtype)
    sin = sin[None, :, None, :].astype(x.dtype)
    return x * cos + rotate_half(x) * sin


def apply_axial(x, tabs):
    cos_r, sin_r, cos_c, sin_c = tabs
    half = x.shape[-1] // 2
    return jnp.concatenate([apply_rotary(x[..., :half], cos_r, sin_r),
                            apply_rotary(x[..., half:], cos_c, sin_c)], axis=-1)


def split_proj(p):
    out, o = [], 0
    for s in SPLIT_SIZES:
        out.append(p[..., o:o + s])
        o += s
    return out


def sq_relu_mlp(h, w1, w2):
    return jnp.square(jax.nn.relu(h @ w1)) @ w2


def retention_chunkwise(q, k, v, log_g, s0):
    B, L, H, DK = q.shape
    DV = v.shape[-1]
    C = RET_CHUNK
    N = L // C
    qc = q.reshape(B, N, C, H, DK)
    kc = k.reshape(B, N, C, H, DK)
    vc = v.reshape(B, N, C, H, DV)
    idx = jnp.arange(C, dtype=F32)
    diff = idx[:, None] - idx[None, :]
    dmask = jnp.where(diff >= 0, jnp.exp(log_g[:, None, None] * jnp.maximum(diff, 0.0)[None]), 0.0)
    scores = jnp.einsum('bnihd,bnjhd->bnhij', qc, kc) * dmask
    o_inner = jnp.einsum('bnhij,bnjhe->bnihe', scores, vc)
    k_dec = jnp.exp(log_g[None, :] * (C - 1 - idx)[:, None])
    kv = jnp.einsum('bnjhd,jh,bnjhe->nbhde', kc, k_dec, vc)
    chunk_dec = jnp.exp(log_g * C)[None, :, None, None]

    def step(s, kv_n):
        return chunk_dec * s + kv_n, s

    _, s_start = lax.scan(step, s0, kv)
    q_dec = jnp.exp(log_g[None, :] * (idx + 1.0)[:, None])
    o_cross = jnp.einsum('bnihd,ih,nbhde->bnihe', qc, q_dec, s_start)
    return (o_inner + o_cross).reshape(B, L, H, DV)


def retention_bidir(q, k, v, log_f, log_b, s_f, s_b):
    fwd = retention_chunkwise(q, k, v, log_f, s_f)
    bwd = retention_chunkwise(jnp.flip(q, 1), jnp.flip(k, 1), jnp.flip(v, 1), log_b, s_b)
    return fwd + jnp.flip(bwd, 1)


def retention_ctx_states(k, v, log_f, log_b):
    m = jnp.arange(k.shape[1], dtype=F32)
    w_f = jnp.exp(log_f[None, :] * (k.shape[1] - 1.0 - m)[:, None])
    w_b = jnp.exp(log_b[None, :] * m[:, None])
    s_f = jnp.einsum('bmhd,mh,bmhe->bhde', k, w_f, v)
    s_b = jnp.einsum('bmhd,mh,bmhe->bhde', k, w_b, v)
    return s_f, s_b


def short_conv(u, w, b):
    up = jnp.pad(u, ((0, 0), (1, 1), (0, 0)))
    return up[:, :-2] * w[0] + up[:, 1:-1] * w[1] + up[:, 2:] * w[2] + b


def hyena_filters(L, w1, b1, w2, b2, w3, freq):
    t = jnp.linspace(0.0, 1.0, L, dtype=F32)[:, None]
    w = 2.0 * math.pi * jnp.arange(L, dtype=F32) / L
    bands = jnp.linspace(1e-4, HY_BANDS - 1, HY_BANDS, dtype=F32)
    ang = w[:, None] * bands[None, :]
    z = jnp.concatenate([t, jnp.cos(ang), -jnp.sin(ang)], axis=-1)
    fr = freq.astype(F32)
    h = jnp.sin(fr * (z @ w1.astype(F32) + b1.astype(F32)))
    for j in range(HY_INNER):
        h = jnp.sin(fr * (h @ w2[j].astype(F32) + b2[j].astype(F32)))
    h = h @ w3.astype(F32)
    max_decay = math.log(HY_TARGET) / HY_FAST_DECAY
    min_decay = math.log(HY_TARGET) / HY_SLOW_DECAY
    deltas = jnp.linspace(min_decay, max_decay, HY_CH, dtype=F32)
    decay = jnp.exp(-t * jnp.abs(deltas)[None, :])
    h = h.reshape(L, 2, HY_CH) * decay[:, None, :]
    return h[:, 0], h[:, 1]


def long_conv(z, h_f, h_b):
    L = z.shape[1]
    n = 2 * L
    filt = jnp.concatenate([h_f, jnp.zeros((1, h_f.shape[1]), F32), h_b[1:][::-1]], axis=0)
    zf = jnp.fft.rfft(z.astype(F32), n=n, axis=1)
    hf = jnp.fft.rfft(filt, n=n, axis=0)
    y = jnp.fft.irfft(zf * hf[None], n=n, axis=1)[:, :L]
    return y.astype(z.dtype)


def hyena_mix(u, short_w, short_b, filt, hy_bias):
    u = short_conv(u, short_w, short_b)
    v, x1, x0 = jnp.split(u, 3, axis=-1)
    h_f, h_b = hyena_filters(u.shape[1], *filt)
    z = v * x1
    y = long_conv(z, h_f, h_b) + z * hy_bias
    return y * x0


def sink_attend(q, k, v, bias, sink):
    s = jnp.einsum('bqhgd,bkhd->bhgqk', q, k).astype(F32) * (ATT_HD ** -0.5) + bias
    sk = sink[None, :, :, None, None]
    m = jnp.maximum(jnp.max(s, axis=-1, keepdims=True), sk)
    p = jnp.exp(s - m)
    den = jnp.sum(p, axis=-1, keepdims=True) + jnp.exp(sk - m)
    return jnp.einsum('bhgqk,bkhd->bqhgd', (p / den).astype(v.dtype), v)


def window_attention(q, k, v, kc, vc, sink):
    B, L = q.shape[0], q.shape[1]
    nb = L // ATT_BLOCK
    nband = 2 * ATT_HALO + 1
    pad = ATT_HALO * ATT_BLOCK
    qb = jnp.moveaxis(q.reshape(B, nb, ATT_BLOCK, ATT_KV_HEADS, ATT_GROUP, ATT_HD), 1, 0)

    def band(t):
        tp = jnp.pad(t, ((0, 0), (pad, pad), (0, 0), (0, 0)))
        blocks = [tp[:, s * ATT_BLOCK: s * ATT_BLOCK + L].reshape(B, nb, ATT_BLOCK, ATT_KV_HEADS, ATT_HD)
                  for s in range(nband)]
        return jnp.moveaxis(jnp.concatenate(blocks, axis=2), 1, 0)

    kb, vb = band(k), band(v)
    ii = jnp.arange(ATT_BLOCK)[:, None]
    jj = jnp.arange(nband * ATT_BLOCK)[None, :]
    in_window = jnp.abs(jj - pad - ii) <= WINDOW
    key_pos = jnp.arange(nb)[:, None] * ATT_BLOCK - pad + jnp.arange(nband * ATT_BLOCK)[None, :]
    in_range = (key_pos >= 0) & (key_pos < L)
    band_bias = jnp.where(in_window[None] & in_range[:, None, :], 0.0, NEG_INF).astype(F32)
    ctx_bias = jnp.zeros((ATT_BLOCK, kc.shape[1]), F32)

    def one_block(args):
        qn, kn, vn, bn = args
        return sink_attend(qn, jnp.concatenate([kn, kc], axis=1), jnp.concatenate([vn, vc], axis=1),
                           jnp.concatenate([bn, ctx_bias], axis=-1), sink)

    ob = lax.map(one_block, (qb, kb, vb, band_bias))
    return jnp.moveaxis(ob, 0, 1).reshape(B, L, ATT_Q)


def trunk_layer(x, xc, sc, scc, axial_tabs, ret_tabs, lp, last):
    (w_ada, b_ada, g_pre_mix, g_post_mix, g_pre_mlp, g_post_mlp, w_in, dec_f, dec_b,
     short_w, short_b, f_w1, f_b1, f_w2, f_b2, f_w3, f_freq, hy_bias, attn_sink,
     g_ret, g_hy, g_att, w_out, w_ff1, w_ff2) = lp
    B, L, _ = x.shape
    Lc = xc.shape[1]
    sh_a, sc_a, gt_a, sh_m, sc_m, gt_m = jnp.split((sc @ w_ada + b_ada)[:, None, :], 6, axis=-1)
    csh_a, csc_a, cgt_a, csh_m, csc_m, cgt_m = jnp.split((scc @ w_ada + b_ada)[None, None, :], 6, axis=-1)
    log_f = jnp.log1p(-jnp.exp(dec_f.astype(F32)))
    log_b = jnp.log1p(-jnp.exp(dec_b.astype(F32)))
    sink = attn_sink.astype(F32).reshape(ATT_KV_HEADS, ATT_GROUP)
    filt = (f_w1, f_b1, f_w2, f_b2, f_w3, f_freq)
    k_scale = RET_DK ** -0.5

    px = modulate(rms_norm(x, g_pre_mix), sh_a, sc_a) @ w_in
    pc = modulate(rms_norm(xc, g_pre_mix), csh_a, csc_a) @ w_in
    rq, rk, rv, rg, hy, aq, ak, av = split_proj(px)
    crq, crk, crv, crg, chy, caq, cak, cav = split_proj(pc)

    crk = crk.reshape(B, Lc, RET_HEADS, RET_DK).astype(F32) * k_scale
    crv = crv.reshape(B, Lc, RET_HEADS, RET_DV).astype(F32)
    s_f, s_b = retention_ctx_states(crk, crv, log_f, log_b)
    rq = apply_rotary(rq.reshape(B, L, RET_HEADS, RET_DK), *ret_tabs).astype(F32)
    rk = apply_rotary(rk.reshape(B, L, RET_HEADS, RET_DK), *ret_tabs).astype(F32) * k_scale
    rv = rv.reshape(B, L, RET_HEADS, RET_DV).astype(F32)
    ret = retention_bidir(rq, rk, rv, log_f, log_b, s_f, s_b)
    ret = rms_norm(ret).reshape(B, L, RET_V).astype(x.dtype) * jax.nn.silu(rg)

    hyo = hyena_mix(hy, short_w, short_b, filt, hy_bias)

    aq = apply_axial(aq.reshape(B, L, ATT_HEADS, ATT_HD), axial_tabs)
    ak = apply_axial(ak.reshape(B, L, ATT_KV_HEADS, ATT_HD), axial_tabs)
    av = av.reshape(B, L, ATT_KV_HEADS, ATT_HD)
    cak = cak.reshape(B, Lc, ATT_KV_HEADS, ATT_HD)
    cav = cav.reshape(B, Lc, ATT_KV_HEADS, ATT_HD)
    att = window_attention(aq, ak, av, cak, cav, sink)

    mix = jnp.concatenate([rms_norm(ret, g_ret), rms_norm(hyo, g_hy), rms_norm(att, g_att)], axis=-1) @ w_out
    x = x + gt_a * rms_norm(mix, g_post_mix)

    if not last:
        zeros = jnp.zeros((B, RET_HEADS, RET_DK, RET_DV), F32)
        crq = crq.reshape(B, Lc, RET_HEADS, RET_DK).astype(F32)
        cret = retention_bidir(crq, crk, crv, log_f, log_b, zeros, zeros)
        cret = rms_norm(cret).reshape(B, Lc, RET_V).astype(xc.dtype) * jax.nn.silu(crg)
        chyo = hyena_mix(chy, short_w, short_b, filt, hy_bias)
        catt = sink_attend(caq.reshape(B, Lc, ATT_KV_HEADS, ATT_GROUP, ATT_HD), cak, cav, 0.0, sink)
        catt = catt.reshape(B, Lc, ATT_Q)
        cmix = jnp.concatenate([rms_norm(cret, g_ret), rms_norm(chyo, g_hy), rms_norm(catt, g_att)], axis=-1) @ w_out
        xc = xc + cgt_a * rms_norm(cmix, g_post_mix)

    hm = sq_relu_mlp(modulate(rms_norm(x, g_pre_mlp), sh_m, sc_m), w_ff1, w_ff2)
    x = x + gt_m * rms_norm(hm, g_post_mlp)
    if not last:
        hcm = sq_relu_mlp(modulate(rms_norm(xc, g_pre_mlp), csh_m, csc_m), w_ff1, w_ff2)
        xc = xc + cgt_m * rms_norm(hcm, g_post_mlp)
    return x, xc


def setup_inputs(seed: int = 0) -> dict:
    key = jax.random.key(seed)
    ks = iter(jax.random.split(key, 32))

    def nrm(shape, scale):
        return scale * jax.random.normal(next(ks), shape, F32)

    ret_base = (-(5.0 + jnp.arange(RET_HEADS, dtype=F32)) * math.log(2.0))[None, :]
    return {
        'x': nrm((BATCH, SEQ, D_MODEL), 1.0),
        'c': nrm((BATCH, D_MODEL), 1.0),
        'ctx': nrm((BATCH, CTX_LEN, D_MODEL), 1.0),
        'c_ctx': nrm((D_MODEL,), 1.0),
        'w_ada': nrm((DEPTH, D_MODEL, 6 * D_MODEL), 0.5 * D_MODEL ** -0.5),
        'b_ada': nrm((DEPTH, 6 * D_MODEL), 0.02),
        'g_pre_mix': 1.0 + nrm((DEPTH, D_MODEL), 0.05),
        'g_post_mix': 1.0 + nrm((DEPTH, D_MODEL), 0.05),
        'g_pre_mlp': 1.0 + nrm((DEPTH, D_MODEL), 0.05),
        'g_post_mlp': 1.0 + nrm((DEPTH, D_MODEL), 0.05),
        'w_in': nrm((DEPTH, D_MODEL, D_IN), D_MODEL ** -0.5),
        'ret_decay_fwd': ret_base + nrm((DEPTH, RET_HEADS), 0.1),
        'ret_decay_bwd': ret_base + nrm((DEPTH, RET_HEADS), 0.1),
        'hy_short_w': nrm((DEPTH, 3, 3 * HY_CH), 3.0 ** -0.5),
        'hy_short_b': nrm((DEPTH, 3 * HY_CH), 0.02),
        'hy_f_w1': nrm((DEPTH, HY_EMB, HY_ORDER), HY_EMB ** -0.5),
        'hy_f_b1': nrm((DEPTH, HY_ORDER), 0.1),
        'hy_f_w2': nrm((DEPTH, HY_INNER, HY_ORDER, HY_ORDER), HY_ORDER ** -0.5),
        'hy_f_b2': nrm((DEPTH, HY_INNER, HY_ORDER), 0.1),
        'hy_f_w3': nrm((DEPTH, HY_ORDER, 2 * HY_CH), HY_ORDER ** -0.5),
        'hy_f_freq': 1.0 + nrm((DEPTH, HY_ORDER), 0.05),
        'hy_bias': nrm((DEPTH, HY_CH), 0.5),
        'attn_sink': nrm((DEPTH, ATT_HEADS), 0.5),
        'g_ret': 1.0 + nrm((DEPTH, RET_V), 0.05),
        'g_hy': 1.0 + nrm((DEPTH, HY_CH), 0.05),
        'g_att': 1.0 + nrm((DEPTH, ATT_Q), 0.05),
        'w_out': nrm((DEPTH, MIX_WIDTH, D_MODEL), MIX_WIDTH ** -0.5),
        'w_ff1': nrm((DEPTH, D_MODEL, D_FF), D_MODEL ** -0.5),
        'w_ff2': nrm((DEPTH, D_FF, D_MODEL), D_FF ** -0.5),
    }


def reference(x, c, ctx, c_ctx, w_ada, b_ada, g_pre_mix, g_post_mix, g_pre_mlp, g_post_mlp,
              w_in, ret_decay_fwd, ret_decay_bwd, hy_short_w, hy_short_b, hy_f_w1, hy_f_b1,
              hy_f_w2, hy_f_b2, hy_f_w3, hy_f_freq, hy_bias, attn_sink, g_ret, g_hy, g_att,
              w_out, w_ff1, w_ff2):
    n_tok = x.shape[1]
    ROWS = n_tok // GRID_W
    pos = jnp.arange(ROWS * GRID_W)
    row = (pos // GRID_W).astype(F32)
    col = (pos % GRID_W).astype(F32)
    half = ATT_HD // 2
    inv_ax = ROPE_BASE ** (-jnp.arange(0, half, 2, dtype=F32) / half)
    cos_r, sin_r = rotary_table(row, inv_ax)
    cos_c, sin_c = rotary_table(col, inv_ax)
    axial_tabs = (cos_r, sin_r, cos_c, sin_c)
    inv_ret = 1.0 / (RET_ROT_BASE ** jnp.linspace(0.0, 1.0, RET_DK // 2, dtype=F32))
    ret_tabs = rotary_table(pos.astype(F32), inv_ret)

    sc = jax.nn.silu(c)
    scc = jax.nn.silu(c_ctx)
    xc = ctx
    for i in range(DEPTH):
        lp = (w_ada[i], b_ada[i], g_pre_mix[i], g_post_mix[i], g_pre_mlp[i], g_post_mlp[i], w_in[i],
              ret_decay_fwd[i], ret_decay_bwd[i], hy_short_w[i], hy_short_b[i], hy_f_w1[i], hy_f_b1[i],
              hy_f_w2[i], hy_f_b2[i], hy_f_w3[i], hy_f_freq[i], hy_bias[i], attn_sink[i],
              g_ret[i], g_hy[i], g_att[i], w_out[i], w_ff1[i], w_ff2[i])
        x, xc = trunk_layer(x, xc, sc, scc, axial_tabs, ret_tabs, lp, i == DEPTH - 1)
    return x
```

```python
import functools
import math

import jax
import jax.numpy as jnp
from jax import lax
from jax.experimental import pallas as pl
from jax.experimental.pallas import tpu as pltpu

F32 = jnp.float32
BF16 = jnp.bfloat16
EPS = 1e-6
NEG_INF = -1e30

RET_HEADS = 4
RET_DK = 32
RET_DV = 64
RET_QK = RET_HEADS * RET_DK
RET_V = RET_HEADS * RET_DV
RET_CHUNK = 128
RET_ROT_BASE = 10000.0
HY_CH = 256
HY_EMB = 33
HY_BANDS = 16
HY_ORDER = 64
HY_FAST_DECAY = 0.3
HY_SLOW_DECAY = 1.5
HY_TARGET = 1e-2
ATT_HEADS = 8
ATT_KV_HEADS = 2
ATT_HD = 64
ATT_Q = ATT_HEADS * ATT_HD
ATT_KV = ATT_KV_HEADS * ATT_HD
WINDOW = 128
ATT_BLOCK = 128
ROPE_BASE = 10000.0
GRID_W = 64

P16_W = 2 * RET_QK + RET_V + ATT_Q + 2 * ATT_KV
P32_W = RET_V + 3 * HY_CH

LANES = 128
VMEM_LIMIT = 56 * 1024 * 1024


def _cparams(sem):
    return pltpu.CompilerParams(dimension_semantics=sem, vmem_limit_bytes=VMEM_LIMIT)


def _const_spec(shape):
    nd = len(shape)
    return pl.BlockSpec(shape, lambda *_: (0,) * nd, pipeline_mode=pl.Buffered(1))


def _rms(x):
    return x * lax.rsqrt(jnp.mean(x * x, axis=-1, keepdims=True) + EPS)


def _dot(a, b):
    return jnp.dot(a, b, preferred_element_type=F32)


def _dot_nt(a, b):
    return lax.dot_general(a, b, (((1,), (1,)), ((), ())), preferred_element_type=F32)


def _dot_tn(a, b):
    return lax.dot_general(a, b, (((0,), (0,)), ((), ())), preferred_element_type=F32)


def _split(a):
    hi = a.astype(BF16)
    lo = (a - hi.astype(F32)).astype(BF16)
    return hi, lo


def _dot3(a, b):
    ah, al = _split(a)
    bh, bl = _split(b)
    return _dot(ah, bh) + _dot(al, bh) + _dot(ah, bl)


def _ada_kernel(c_ref, w_ref, b_ref, o_ref):
    cv = c_ref[...]
    s = cv * jax.nn.sigmoid(cv)
    o_ref[0] = _dot(s.astype(BF16), w_ref[0].astype(BF16)) + b_ref[0]


def _ada(cc, w_ada, b_ada):
    depth, d, d6 = w_ada.shape
    rows = cc.shape[0]
    tn = 1536
    return pl.pallas_call(
        _ada_kernel,
        grid=(depth, d6 // tn),
        in_specs=[
            pl.BlockSpec((rows, d), lambda l, j: (0, 0)),
            pl.BlockSpec((1, d, tn), lambda l, j: (l, 0, j)),
            pl.BlockSpec((1, 1, tn), lambda l, j: (l, 0, j)),
        ],
        out_specs=pl.BlockSpec((1, rows, tn), lambda l, j: (l, 0, j)),
        out_shape=jax.ShapeDtypeStruct((depth, rows, d6), F32),
        compiler_params=_cparams(("arbitrary", "arbitrary")),
        name="ada",
    )(cc, w_ada, b_ada.reshape(depth, 1, d6))


def _rot(x, cos, sa, sb):
    return x * cos + pltpu.roll(x, LANES - 16, 1) * sa + pltpu.roll(x, 16, 1) * sb


def _in_kernel(x_ref, sh_ref, sc_ref, g_ref, w_ref, *rest, rotary):
    if rotary:
        rc, rsa, rsb, ac, asa, asb, o16_ref, o32_ref = rest
    else:
        o16_ref, o32_ref = rest
    x = x_ref[0]
    h = (_rms(x) * g_ref[...]) * (1.0 + sc_ref[0]) + sh_ref[0]
    hb = h.astype(BF16)
    k_scale = RET_DK ** -0.5
    q_scale = ATT_HD ** -0.5

    pr = _dot(hb, w_ref[:, 0:512])
    rq, rk, rv = pr[:, 0:128], pr[:, 128:256], pr[:, 256:512]
    if rotary:
        rq = _rot(rq, rc[...], rsa[...], rsb[...])
        rk = _rot(rk, rc[...], rsa[...], rsb[...])
    o16_ref[0, :, 0:128] = rq.astype(BF16)
    o16_ref[0, :, 128:256] = (rk * k_scale).astype(BF16)
    o16_ref[0, :, 256:512] = rv.astype(BF16)

    o32_ref[0] = _dot(hb, w_ref[:, 512:1536])

    pa = _dot(hb, w_ref[:, 1536:2304])
    for j in range(4):
        aq = pa[:, 128 * j:128 * (j + 1)]
        if rotary:
            aq = _rot(aq, ac[...], asa[...], asb[...])
        o16_ref[0, :, 512 + 128 * j:640 + 128 * j] = (aq * q_scale).astype(BF16)
    ak = pa[:, 512:640]
    if rotary:
        ak = _rot(ak, ac[...], asa[...], asb[...])
    o16_ref[0, :, 1024:1152] = ak.astype(BF16)
    o16_ref[0, :, 1152:1280] = pa[:, 640:768].astype(BF16)


def _in_proj(x3, mods, mod_row, g_pre, w_in16, tabs, tm):
    G, R, D = x3.shape
    rotary = tabs is not None
    nt = R // tm
    in_specs = [
        pl.BlockSpec((1, tm, D), lambda g, i: (g, i, 0)),
        pl.BlockSpec((1, 1, D), lambda g, i: (mod_row(g), 0, 0)),
        pl.BlockSpec((1, 1, D), lambda g, i: (mod_row(g), 0, 1)),
        _const_spec((1, D)),
        _const_spec(w_in16.shape),
    ]
    args = [x3, mods, mods, g_pre, w_in16]
    if rotary:
        in_specs += [pl.BlockSpec((tm, LANES), lambda g, i: (i, 0))] * 6
        args += list(tabs)
    return pl.pallas_call(
        functools.partial(_in_kernel, rotary=rotary),
        grid=(G, nt),
        in_specs=in_specs,
        out_specs=[
            pl.BlockSpec((1, tm, P16_W), lambda g, i: (g, i, 0)),
            pl.BlockSpec((1, tm, P32_W), lambda g, i: (g, i, 0)),
        ],
        out_shape=[
            jax.ShapeDtypeStruct((G, R, P16_W), BF16),
            jax.ShapeDtypeStruct((G, R, P32_W), F32),
        ],
        compiler_params=_cparams(("parallel", "parallel")),
        name="in_proj_rot" if rotary else "in_proj",
    )(*args)


def _ret_kernel(qkv_ref, rg_ref, dq_ref, ds_ref, dv_ref, g_ref, *rest, L, has_init):
    if has_init:
        s0_ref, o_ref, sfin_ref, sb_scr = rest
    else:
        o_ref, sfin_ref, sb_scr = rest
    C = RET_CHUNK
    N = L // C

    def log_gamma(ref, r):
        return jnp.log1p(-jnp.exp(ref[r:r + 1, :]))

    lfq, lbq = log_gamma(dq_ref, 0), log_gamma(dq_ref, 1)
    lfs, lbs = log_gamma(ds_ref, 0), log_gamma(ds_ref, 1)
    lfv, lbv = log_gamma(dv_ref, 0), log_gamma(dv_ref, 1)
    ri = lax.broadcasted_iota(jnp.int32, (C, LANES), 0).astype(F32)
    qdec_f = jnp.exp(lfq * (ri + 1.0))
    kdec_f = jnp.exp(lfq * (C - 1.0 - ri))
    qdec_b = jnp.exp(lbq * (C - ri))
    kdec_b = jnp.exp(lbq * ri)
    di = lax.broadcasted_iota(jnp.int32, (C, 4 * C), 0)
    dj = lax.broadcasted_iota(jnp.int32, (C, 4 * C), 1) & (C - 1)
    diff = (di - dj).astype(F32)
    dmask = (jnp.where(diff >= 0, jnp.exp(lfs * jnp.maximum(diff, 0.0)), 0.0)
             + jnp.where(diff <= 0, jnp.exp(lbs * jnp.maximum(-diff, 0.0)), 0.0))
    cdec_f = jnp.exp(lfv * float(C))
    cdec_b = jnp.exp(lbv * float(C))

    lane_q = lax.broadcasted_iota(jnp.int32, (1, RET_QK), 1) >> 5
    lane_v = lax.broadcasted_iota(jnp.int32, (1, RET_V), 1) >> 6
    hm = [jnp.where(lane_q == h, 1.0, 0.0).astype(BF16) for h in range(RET_HEADS)]
    cm = [jnp.where(lane_v == h, 1.0, 0.0).astype(BF16) for h in range(RET_HEADS)]
    bd = ((lax.broadcasted_iota(jnp.int32, (RET_QK, RET_V), 0) >> 5)
          == (lax.broadcasted_iota(jnp.int32, (RET_QK, RET_V), 1) >> 6))
    ones64 = jnp.where((lax.broadcasted_iota(jnp.int32, (RET_V, RET_V), 0) >> 6)
                       == (lax.broadcasted_iota(jnp.int32, (RET_V, RET_V), 1) >> 6), 1.0, 0.0).astype(BF16)

    if has_init:
        sf0 = s0_ref[0, 0]
        sb0 = s0_ref[0, 1]
    else:
        sf0 = jnp.zeros((RET_QK, RET_V), F32)
        sb0 = jnp.zeros((RET_QK, RET_V), F32)

    def kv_update(k16, v16, kdec, cdec, s):
        kd = (k16.astype(F32) * kdec).astype(BF16)
        return cdec * s + jnp.where(bd, _dot_tn(kd, v16), 0.0)

    def bwd_body(t, sb):
        n = N - 1 - t
        r0 = pl.multiple_of(n * C, C)
        sb_scr[n] = sb
        return kv_update(qkv_ref[0, pl.ds(r0, C), 128:256], qkv_ref[0, pl.ds(r0, C), 256:512],
                         kdec_b, cdec_b, sb)

    sb_fin = lax.fori_loop(0, N, bwd_body, sb0)

    def fwd_body(n, sf):
        r0 = pl.multiple_of(n * C, C)
        q16 = qkv_ref[0, pl.ds(r0, C), 0:128]
        k16 = qkv_ref[0, pl.ds(r0, C), 128:256]
        v16 = qkv_ref[0, pl.ds(r0, C), 256:512]
        qf = q16.astype(F32)
        kblk = jnp.concatenate([k16 * hm[h] for h in range(RET_HEADS)], axis=0)
        vblk = jnp.concatenate([v16 * cm[h] for h in range(RET_HEADS)], axis=0)
        p = (_dot_nt(q16, kblk) * dmask).astype(BF16)
        o = (_dot(p, vblk)
             + _dot((qf * qdec_f).astype(BF16), sf.astype(BF16))
             + _dot((qf * qdec_b).astype(BF16), sb_scr[n].astype(BF16)))
        o2h, o2l = _split(o * o)
        ms = (_dot(o2h, ones64) + _dot(o2l, ones64)) * (1.0 / RET_DV)
        rg = rg_ref[0, pl.ds(r0, C), :]
        gated = (o * lax.rsqrt(ms + EPS)) * (rg * jax.nn.sigmoid(rg))
        o_ref[0, pl.ds(r0, C), :] = (_rms(gated) * g_ref[...]).astype(BF16)
        return kv_update(k16, v16, kdec_f, cdec_f, sf)

    sf_fin = lax.fori_loop(0, N, fwd_body, sf0)
    sfin_ref[0, 0] = sf_fin
    sfin_ref[0, 1] = sb_fin


def _retention(p16, p32, dq, ds, dv, g_ret, s0):
    B, L, _ = p16.shape
    has_init = s0 is not None
    in_specs = [
        pl.BlockSpec((1, L, 512), lambda b: (b, 0, 0)),
        pl.BlockSpec((1, L, RET_V), lambda b: (b, 0, 0)),
        _const_spec(dq.shape), _const_spec(ds.shape), _const_spec(dv.shape),
        _const_spec((1, RET_V)),
    ]
    args = [p16, p32, dq, ds, dv, g_ret]
    if has_init:
        in_specs.append(pl.BlockSpec((1, 2, RET_QK, RET_V), lambda b: (b, 0, 0, 0)))
        args.append(s0)
    return pl.pallas_call(
        functools.partial(_ret_kernel, L=L, has_init=has_init),
        grid=(B,),
        in_specs=in_specs,
        out_specs=[
            pl.BlockSpec((1, L, RET_V), lambda b: (b, 0, 0)),
            pl.BlockSpec((1, 2, RET_QK, RET_V), lambda b: (b, 0, 0, 0)),
        ],
        out_shape=[
            jax.ShapeDtypeStruct((B, L, RET_V), BF16),
            jax.ShapeDtypeStruct((B, 2, RET_QK, RET_V), F32),
        ],
        scratch_shapes=[pltpu.VMEM((L // RET_CHUNK, RET_QK, RET_V), F32)],
        compiler_params=_cparams(("parallel",)),
        name="retention_init" if has_init else "retention",
    )(*args)


def _filt_kernel(zf_ref, w1_ref, b1_ref, w2_ref, b2_ref, w3_ref, fr_ref, dec_ref,
                 ch_ref, cl_ref, sh_ref, sl_ref, hr_ref, hi_ref, hn_ref,
                 ah_scr, al_scr, dh_scr, dl_scr, *, L, FC):
    i = pl.program_id(0)
    n = 2 * L

    @pl.when(i == 0)
    def _():
        fr = fr_ref[...]
        h = jnp.sin(fr * (_dot3(zf_ref[...], w1_ref[...]) + b1_ref[...]))
        for j in range(2):
            h = jnp.sin(fr * (_dot3(h, w2_ref[j]) + b2_ref[j]))
        h = _dot3(h, w3_ref[...])
        dec = dec_ref[...]
        row = lax.broadcasted_iota(jnp.int32, (L, HY_CH), 0)
        hf = h[:, 0:HY_CH] * dec
        hb = jnp.where(row == 0, 0.0, h[:, HY_CH:2 * HY_CH] * dec)
        a = hf + hb
        d = hf - hb
        sgn = jnp.where((row & 1) == 1, -1.0, 1.0)
        hn_ref[...] = jnp.sum(a * sgn, axis=0, keepdims=True) * (1.0 / n)
        ah, al = _split(a)
        dh, dl = _split(d)
        ah_scr[...] = ah
        al_scr[...] = al
        dh_scr[...] = dh
        dl_scr[...] = dl

    krow = lax.broadcasted_iota(jnp.int32, (FC, 1), 0) + i * FC
    wk = jnp.where(krow == 0, 1.0 / n, 2.0 / n)
    ch, cl, sh, sl = ch_ref[...], cl_ref[...], sh_ref[...], sl_ref[...]
    ah, al, dh, dl = ah_scr[...], al_scr[...], dh_scr[...], dl_scr[...]
    hr_ref[...] = (_dot(ch, ah) + _dot(ch, al) + _dot(cl, ah)) * wk
    hi_ref[...] = (_dot(sh, dh) + _dot(sh, dl) + _dot(sl, dh)) * wk


def _hyena_spectrum(L, zf, w1, b1, w2, b2, w3, fr, dec, mats):
    ch, cl, sh, sl = mats
    FC = min(512, L)
    mat_spec = pl.BlockSpec((FC, L), lambda i: (i, 0))
    out_spec = pl.BlockSpec((FC, HY_CH), lambda i: (i, 0))
    return pl.pallas_call(
        functools.partial(_filt_kernel, L=L, FC=FC),
        grid=(L // FC,),
        in_specs=[_const_spec(a.shape) for a in (zf, w1, b1, w2, b2, w3, fr, dec)] + [mat_spec] * 4,
        out_specs=[out_spec, out_spec, pl.BlockSpec((1, HY_CH), lambda i: (0, 0))],
        out_shape=[
            jax.ShapeDtypeStruct((L, HY_CH), F32),
            jax.ShapeDtypeStruct((L, HY_CH), F32),
            jax.ShapeDtypeStruct((1, HY_CH), F32),
        ],
        scratch_shapes=[pltpu.VMEM((L, HY_CH), BF16)] * 4,
        compiler_params=_cparams(("arbitrary",)),
        name="hyena_filter",
    )(zf, w1, b1, w2, b2, w3, fr, dec, ch, cl, sh, sl)


def _hy_kernel(v_ref, x1_ref, x0_ref, sw_ref, sb_ref, bias_ref, g_ref, c_ref, s_ref,
               hr_ref, hi_ref, hn_ref, o_ref, z_scr, x0_scr, zb_scr, yr_scr, yi_scr, *, L, R):
    nchunks = L // R
    zn = jnp.zeros((1, HY_CH), F32)
    for c in range(nchunks):
        r0 = c * R
        lo = max(r0 - 8, 0)
        hi = min(r0 + R + 8, L)
        rows = hi - lo
        off = r0 - lo
        grow = lax.broadcasted_iota(jnp.int32, (R, HY_CH), 0) + r0

        def conv(ref, c0):
            ext = ref[0, lo:hi, :]
            up = jnp.where(grow == 0, 0.0, pltpu.roll(ext, 1, 0)[off:off + R])
            un = jnp.where(grow == L - 1, 0.0, pltpu.roll(ext, rows - 1, 0)[off:off + R])
            u = ref[0, r0:r0 + R, :]
            return (up * sw_ref[0:1, c0:c0 + HY_CH] + u * sw_ref[1:2, c0:c0 + HY_CH]
                    + un * sw_ref[2:3, c0:c0 + HY_CH] + sb_ref[:, c0:c0 + HY_CH])

        z = conv(v_ref, 0) * conv(x1_ref, HY_CH)
        z_scr[r0:r0 + R, :] = z
        zb_scr[r0:r0 + R, :] = z.astype(BF16)
        x0_scr[r0:r0 + R, :] = conv(x0_ref, 2 * HY_CH)
        sgn = jnp.where((grow & 1) == 1, -1.0, 1.0)
        zn = zn + jnp.sum(z * sgn, axis=0, keepdims=True)

    zb = zb_scr[...]
    for c in range(nchunks):
        r0 = c * R
        zr = _dot(c_ref[r0:r0 + R, :], zb)
        zi = _dot(s_ref[r0:r0 + R, :], zb)
        hr = hr_ref[r0:r0 + R, :]
        hi_ = hi_ref[r0:r0 + R, :]
        yr_scr[r0:r0 + R, :] = (zr * hr - zi * hi_).astype(BF16)
        yi_scr[r0:r0 + R, :] = (zr * hi_ + zi * hr).astype(BF16)

    yr = yr_scr[...]
    yi = yi_scr[...]
    nyq = zn * hn_ref[...]
    for c in range(nchunks):
        r0 = c * R
        y = _dot(c_ref[r0:r0 + R, :], yr) + _dot(s_ref[r0:r0 + R, :], yi)
        trow = lax.broadcasted_iota(jnp.int32, (R, HY_CH), 0) + r0
        y = y + jnp.where((trow & 1) == 1, -nyq, nyq)
        y = y + z_scr[r0:r0 + R, :] * bias_ref[...]
        out = y * x0_scr[r0:r0 + R, :]
        o_ref[0, r0:r0 + R, :] = (_rms(out) * g_ref[...]).astype(BF16)


def _hyena(p32, short_w, short_b, hy_bias, g_hy, cmat, smat, hr, hi, hn):
    B, L, _ = p32.shape
    R = min(256, L)
    col = lambda j: pl.BlockSpec((1, L, HY_CH), lambda b: (b, 0, j))
    return pl.pallas_call(
        functools.partial(_hy_kernel, L=L, R=R),
        grid=(B,),
        in_specs=[col(1), col(2), col(3)] + [
            _const_spec(a.shape) for a in (short_w, short_b, hy_bias, g_hy, cmat, smat, hr, hi, hn)],
        out_specs=pl.BlockSpec((1, L, HY_CH), lambda b: (b, 0, 0)),
        out_shape=jax.ShapeDtypeStruct((B, L, HY_CH), BF16),
        scratch_shapes=[
            pltpu.VMEM((L, HY_CH), F32), pltpu.VMEM((L, HY_CH), F32),
            pltpu.VMEM((L, HY_CH), BF16), pltpu.VMEM((L, HY_CH), BF16), pltpu.VMEM((L, HY_CH), BF16),
        ],
        compiler_params=_cparams(("parallel",)),
        name="hyena",
    )(p32, p32, p32, short_w, short_b, hy_bias, g_hy, cmat, smat, hr, hi, hn)


def _expand_kv(kv_ref, k2_scr, va_scr, vb_scr, rows):
    R = min(256, rows)
    lo = lax.broadcasted_iota(jnp.int32, (R, LANES), 1) < ATT_HD
    for c in range(rows // R):
        sl = slice(c * R, (c + 1) * R)
        k = kv_ref[0, sl, 0:128].astype(F32)
        v = kv_ref[0, sl, 128:256].astype(F32)
        kr = pltpu.roll(k, ATT_HD, 1)
        vr = pltpu.roll(v, ATT_HD, 1)
        k2_scr[0, sl, :] = jnp.where(lo, k, kr).astype(BF16)
        k2_scr[1, sl, :] = jnp.where(lo, kr, k).astype(BF16)
        va_scr[0, sl, :] = jnp.where(lo, v, 0.0).astype(BF16)
        vb_scr[0, sl, :] = jnp.where(lo, 0.0, vr).astype(BF16)
        va_scr[1, sl, :] = jnp.where(lo, vr, 0.0).astype(BF16)
        vb_scr[1, sl, :] = jnp.where(lo, 0.0, v).astype(BF16)


def _att_kernel(sink_ref, q_ref, *rest, L, Lc, band):
    if band:
        (kv_ref, ckv_ref, g_ref, o_ref, k2_scr, va_scr, vb_scr, ck2_scr, cva_scr, cvb_scr, att_scr) = rest
        _expand_kv(kv_ref, k2_scr, va_scr, vb_scr, L)
    else:
        (ckv_ref, g_ref, o_ref, ck2_scr, cva_scr, cvb_scr, att_scr) = rest
    _expand_kv(ckv_ref, ck2_scr, cva_scr, cvb_scr, Lc)

    T = ATT_BLOCK
    W3 = 3 * T
    lane = lax.broadcasted_iota(jnp.int32, (1, LANES), 1)
    lo16 = jnp.where(lane < ATT_HD, 1.0, 0.0).astype(BF16)
    hi16 = jnp.where(lane >= ATT_HD, 1.0, 0.0).astype(BF16)
    hrow = lax.broadcasted_iota(jnp.int32, (4 * T, 1), 0) >> 7

    def block(n, carry):
        r0 = pl.multiple_of(n * T, T)
        qb = q_ref[0, pl.ds(r0, T), :]
        if band:
            s0 = pl.multiple_of(jnp.clip((n - 1) * T, 0, L - W3), T)
            bi = lax.broadcasted_iota(jnp.int32, (4 * T, W3), 0) & (T - 1)
            bj = lax.broadcasted_iota(jnp.int32, (4 * T, W3), 1)
            allowed4 = jnp.abs(bj + (s0 - r0) - bi) <= WINDOW
        for g in range(ATT_KV_HEADS):
            qs = []
            for p in range(2):
                qp = qb[:, 128 * (2 * g + p):128 * (2 * g + p + 1)]
                qs += [qp * lo16, qp * hi16]
            qst = jnp.concatenate(qs, axis=0)
            sk = jnp.where(hrow == 0, sink_ref[4 * g],
                           jnp.where(hrow == 1, sink_ref[4 * g + 1],
                                     jnp.where(hrow == 2, sink_ref[4 * g + 2], sink_ref[4 * g + 3])))
            sc = _dot_nt(qst, ck2_scr[g])
            m = jnp.maximum(jnp.max(sc, axis=-1, keepdims=True), sk)
            if band:
                sbd = jnp.where(allowed4, _dot_nt(qst, k2_scr[g, pl.ds(s0, W3), :]), NEG_INF)
                m = jnp.maximum(m, jnp.max(sbd, axis=-1, keepdims=True))
                pb = jnp.exp(sbd - m)
            pc = jnp.exp(sc - m)
            den = jnp.sum(pc, axis=-1, keepdims=True) + jnp.exp(sk - m)
            if band:
                den = den + jnp.sum(pb, axis=-1, keepdims=True)
                pb = pb.astype(BF16)
            pc = pc.astype(BF16)
            inv = 1.0 / den
            for p in range(2):
                re = slice(2 * p * T, (2 * p + 1) * T)
                ro = slice((2 * p + 1) * T, (2 * p + 2) * T)
                oe = _dot(pc[re], cva_scr[g])
                oo = _dot(pc[ro], cvb_scr[g])
                if band:
                    oe = oe + _dot(pb[re], va_scr[g, pl.ds(s0, W3), :])
                    oo = oo + _dot(pb[ro], vb_scr[g, pl.ds(s0, W3), :])
                c0 = 128 * (2 * g + p)
                att_scr[:, c0:c0 + 128] = oe * inv[re] + oo * inv[ro]
        o_ref[0, pl.ds(r0, T), :] = (_rms(att_scr[...]) * g_ref[...]).astype(BF16)
        return carry

    lax.fori_loop(0, L // T, block, 0)


def _attention(sink, p16, pc16, g_att, band):
    B, Lc, _ = pc16.shape
    L = p16.shape[1] if band else Lc
    qsrc = p16 if band else pc16
    kv_spec = lambda n: pl.BlockSpec((1, n, 256), lambda b: (b, 0, 4))
    in_specs = [pl.BlockSpec(memory_space=pltpu.SMEM),
                pl.BlockSpec((1, L, ATT_Q), lambda b: (b, 0, 1))]
    args = [sink, qsrc]
    scratch = []
    if band:
        in_specs.append(kv_spec(L))
        args.append(p16)
        scratch += [pltpu.VMEM((2, L, LANES), BF16)] * 3
    in_specs += [kv_spec(Lc), _const_spec((1, ATT_Q))]
    args += [pc16, g_att]
    scratch += [pltpu.VMEM((2, Lc, LANES), BF16)] * 3 + [pltpu.VMEM((ATT_BLOCK, ATT_Q), F32)]
    return pl.pallas_call(
        functools.partial(_att_kernel, L=L, Lc=Lc, band=band),
        grid=(B,),
        in_specs=in_specs,
        out_specs=pl.BlockSpec((1, L, ATT_Q), lambda b: (b, 0, 0)),
        out_shape=jax.ShapeDtypeStruct((B, L, ATT_Q), BF16),
        scratch_shapes=scratch,
        compiler_params=_cparams(("parallel",)),
        name="attn_window" if band else "attn_ctx",
    )(*args)


def _out_mlp_kernel(x_ref, nr_ref, nh_ref, na_ref, gta_ref, shm_ref, scm_ref, gtm_ref,
                    gpm_ref, gpre_ref, gpost_ref, wo_ref, w1_ref, w2_ref, o_ref, *, ff_chunk):
    x = x_ref[0]
    mix = (_dot(nr_ref[0], wo_ref[0:256, :]) + _dot(nh_ref[0], wo_ref[256:512, :])
           + _dot(na_ref[0], wo_ref[512:1024, :]))
    x1 = x + gta_ref[0] * (_rms(mix) * gpm_ref[...])
    h = ((_rms(x1) * gpre_ref[...]) * (1.0 + scm_ref[0]) + shm_ref[0]).astype(BF16)
    dff = w1_ref.shape[1]
    acc = None
    for j in range(dff // ff_chunk):
        sl = slice(j * ff_chunk, (j + 1) * ff_chunk)
        hj = jnp.square(jnp.maximum(_dot(h, w1_ref[:, sl]), 0.0)).astype(BF16)
        part = _dot(hj, w2_ref[sl, :])
        acc = part if acc is None else acc + part
    o_ref[0] = x1 + gtm_ref[0] * (_rms(acc) * gpost_ref[...])


def _out_mlp(x3, nr, nh, na, mods, mod_row, g_post_mix, g_pre_mlp, g_post_mlp, wo16, w116, w216, tm):
    G, R, D = x3.shape
    row = lambda w: pl.BlockSpec((1, tm, w), lambda g, i: (g, i, 0))
    mod = lambda j: pl.BlockSpec((1, 1, D), lambda g, i: (mod_row(g), 0, j))
    return pl.pallas_call(
        functools.partial(_out_mlp_kernel, ff_chunk=1024),
        grid=(G, R // tm),
        in_specs=[row(D), row(RET_V), row(HY_CH), row(ATT_Q), mod(2), mod(3), mod(4), mod(5),
                  _const_spec((1, D)), _const_spec((1, D)), _const_spec((1, D)),
                  _const_spec(wo16.shape), _const_spec(w116.shape), _const_spec(w216.shape)],
        out_specs=row(D),
        out_shape=jax.ShapeDtypeStruct((G, R, D), F32),
        compiler_params=_cparams(("parallel", "parallel")),
        name="out_mlp",
    )(x3, nr, nh, na, mods, mods, mods, mods, g_post_mix, g_pre_mlp, g_post_mlp, wo16, w116, w216)


def _rot_tables(cos, sin):
    lane = jnp.arange(LANES) % 32
    first = (lane < 16)[None, :]
    return cos, jnp.where(first, -sin, 0.0), jnp.where(first, 0.0, sin)


def _rotary_tables(L):
    pos = jnp.arange(L)
    row = (pos // GRID_W).astype(F32)
    col = (pos % GRID_W).astype(F32)
    half = ATT_HD // 2
    inv_ax = ROPE_BASE ** (-jnp.arange(0, half, 2, dtype=F32) / half)

    def tab(p, inv):
        ang = p[:, None] * inv[None, :]
        ang = jnp.concatenate([ang, ang], axis=-1)
        return jnp.cos(ang), jnp.sin(ang)

    cr, sr = tab(row, inv_ax)
    cc, sc = tab(col, inv_ax)
    inv_ret = 1.0 / (RET_ROT_BASE ** jnp.linspace(0.0, 1.0, RET_DK // 2, dtype=F32))
    ct, st = tab(pos.astype(F32), inv_ret)
    ret = _rot_tables(jnp.tile(ct, (1, 4)), jnp.tile(st, (1, 4)))
    ax = _rot_tables(jnp.tile(jnp.concatenate([cr, cc], -1), (1, 2)),
                     jnp.tile(jnp.concatenate([sr, sc], -1), (1, 2)))
    return ret + ax


def _dft_mats(L):
    n = 2 * L
    k = jnp.arange(L, dtype=jnp.int32)
    ang = ((k[:, None] * k[None, :]) % n).astype(F32) * (2.0 * math.pi / n)
    c = jnp.cos(ang)
    s = -jnp.sin(ang)
    ch, sh = c.astype(BF16), s.astype(BF16)
    cl = (c - ch.astype(F32)).astype(BF16)
    sl = (s - sh.astype(F32)).astype(BF16)
    return ch, cl, sh, sl


def _filter_features(L):
    t = jnp.linspace(0.0, 1.0, L, dtype=F32)[:, None]
    w = 2.0 * math.pi * jnp.arange(L, dtype=F32) / L
    bands = jnp.linspace(1e-4, HY_BANDS - 1, HY_BANDS, dtype=F32)
    ang = w[:, None] * bands[None, :]
    z = jnp.concatenate([t, jnp.cos(ang), -jnp.sin(ang)], axis=-1)
    zf = jnp.pad(z, ((0, 0), (0, LANES - HY_EMB)))
    max_decay = math.log(HY_TARGET) / HY_FAST_DECAY
    min_decay = math.log(HY_TARGET) / HY_SLOW_DECAY
    deltas = jnp.linspace(min_decay, max_decay, HY_CH, dtype=F32)
    dec = jnp.exp(-t * jnp.abs(deltas)[None, :])
    return zf, dec


def _pad_to(a, shape):
    return jnp.pad(a, [(0, s - d) for d, s in zip(a.shape, shape)])


def kernel(x, c, ctx, c_ctx, w_ada, b_ada, g_pre_mix, g_post_mix, g_pre_mlp, g_post_mlp, w_in,
           ret_decay_fwd, ret_decay_bwd, hy_short_w, hy_short_b, hy_f_w1, hy_f_b1, hy_f_w2, hy_f_b2,
           hy_f_w3, hy_f_freq, hy_bias, attn_sink, g_ret, g_hy, g_att, w_out, w_ff1, w_ff2):
    B, L, D = x.shape
    Lc = ctx.shape[1]
    depth = w_ada.shape[0]
    assert D == 1024 and w_in.shape[2] == 2304 and L % 256 == 0 and Lc % 256 == 0 and L >= 3 * ATT_BLOCK

    rows = -(-(B + 1) // 8) * 8
    cc = _pad_to(jnp.concatenate([c, c_ctx[None, :]], axis=0), (rows, D))
    mods_all = _ada(cc, w_ada, b_ada).reshape(depth, rows, 1, 6 * D)

    w_in16 = w_in.astype(BF16)
    w_out16 = w_out.astype(BF16)
    w_ff116 = w_ff1.astype(BF16)
    w_ff216 = w_ff2.astype(BF16)

    tabs = _rotary_tables(L)
    mats = {n: _dft_mats(n) for n in {L, Lc}}
    feats = {n: _filter_features(n) for n in {L, Lc}}

    tm = 512
    ctx_rows = B * Lc
    xc = ctx.reshape(ctx_rows // tm, tm, D)
    lat_row = lambda g: g
    ctx_row = lambda g: B

    for l in range(depth):
        last = l == depth - 1
        mods = mods_all[l]
        dec2 = jnp.stack([ret_decay_fwd[l], ret_decay_bwd[l]])
        dq = jnp.repeat(dec2, RET_DK, axis=1)
        ds = jnp.repeat(dec2, RET_CHUNK, axis=1)
        dv = jnp.repeat(dec2, RET_DV, axis=1)
        gr, gh, ga = g_ret[l][None], g_hy[l][None], g_att[l][None]
        gpre, gpm = g_pre_mix[l][None], g_post_mix[l][None]
        gprm, gpom = g_pre_mlp[l][None], g_post_mlp[l][None]
        fw1 = _pad_to(hy_f_w1[l], (LANES, LANES))
        fb1 = _pad_to(hy_f_b1[l][None], (1, LANES))
        fw2 = _pad_to(hy_f_w2[l], (2, LANES, LANES))
        fb2 = _pad_to(hy_f_b2[l][:, None, :], (2, 1, LANES))
        fw3 = _pad_to(hy_f_w3[l], (LANES, 2 * HY_CH))
        ffr = _pad_to(hy_f_freq[l][None], (1, LANES))
        sw, sb, hbias = hy_short_w[l], hy_short_b[l][None], hy_bias[l][None]

        def spectrum(n):
            zf, dec = feats[n]
            return _hyena_spectrum(n, zf, fw1, fb1, fw2, fb2, fw3, ffr, dec, mats[n])

        pc16, pc32 = _in_proj(xc, mods, ctx_row, gpre, w_in16[l], None, tm)
        pc16 = pc16.reshape(B, Lc, P16_W)
        pc32 = pc32.reshape(B, Lc, P32_W)
        cret, cstate = _retention(pc16, pc32, dq, ds, dv, gr, None)

        p16, p32 = _in_proj(x, mods, lat_row, gpre, w_in16[l], tabs, tm)
        ret, _ = _retention(p16, p32, dq, ds, dv, gr, cstate)
        hr, hi, hn = spectrum(L)
        hyo = _hyena(p32, sw, sb, hbias, gh, mats[L][0], mats[L][2], hr, hi, hn)
        att = _attention(attn_sink[l], p16, pc16, ga, True)
        x = _out_mlp(x, ret, hyo, att, mods, lat_row, gpm, gprm, gpom,
                     w_out16[l], w_ff116[l], w_ff216[l], tm)

        if not last:
            chr_, chi, chn = spectrum(Lc)
            chyo = _hyena(pc32, sw, sb, hbias, gh, mats[Lc][0], mats[Lc][2], chr_, chi, chn)
            catt = _attention(attn_sink[l], None, pc16, ga, False)
            r3 = lambda a: a.reshape(ctx_rows // tm, tm, a.shape[-1])
            xc = _out_mlp(xc, r3(cret), r3(chyo), r3(catt), mods, ctx_row, gpm, gprm, gpom,
                          w_out16[l], w_ff116[l], w_ff216[l], tm)
    return x
```

```python
import functools
import math

import jax
import jax.numpy as jnp
from jax import lax
from jax.experimental import pallas as pl
from jax.experimental.pallas import tpu as pltpu

F32 = jnp.float32
BF16 = jnp.bfloat16
EPS = 1e-6
NEG_INF = -1e30
LOG2E = math.log2(math.e)

RET_HEADS = 4
RET_DK = 32
RET_DV = 64
RET_QK = RET_HEADS * RET_DK
RET_V = RET_HEADS * RET_DV
RET_CHUNK = 128
RET_UNROLL = 4
RET_ROT_BASE = 10000.0
HY_CH = 256
HY_EMB = 33
HY_BANDS = 16
HY_ORDER = 64
HY_FAST_DECAY = 0.3
HY_SLOW_DECAY = 1.5
HY_TARGET = 1e-2
ATT_HEADS = 8
ATT_KV_HEADS = 2
ATT_HD = 64
ATT_Q = ATT_HEADS * ATT_HD
ATT_KV = ATT_KV_HEADS * ATT_HD
WINDOW = 128
ATT_BLOCK = 128
ROPE_BASE = 10000.0
GRID_W = 64

P16_W = 2 * RET_QK + RET_V + ATT_Q + 2 * ATT_KV
P32_W = RET_V + 3 * HY_CH

LANES = 128
VMEM_LIMIT = 56 * 1024 * 1024


def _cparams(sem):
    return pltpu.CompilerParams(dimension_semantics=sem, vmem_limit_bytes=VMEM_LIMIT)


def _const_spec(shape):
    nd = len(shape)
    return pl.BlockSpec(shape, lambda *_: (0,) * nd, pipeline_mode=pl.Buffered(1))


def _rms(x):
    return x * lax.rsqrt(jnp.mean(x * x, axis=-1, keepdims=True) + EPS)


def _dot(a, b):
    return jnp.dot(a, b, preferred_element_type=F32)


def _dot_nt(a, b):
    return lax.dot_general(a, b, (((1,), (1,)), ((), ())), preferred_element_type=F32)


def _dot_tn(a, b):
    return lax.dot_general(a, b, (((0,), (0,)), ((), ())), preferred_element_type=F32)


def _split(a):
    hi = a.astype(BF16)
    lo = (a - hi.astype(F32)).astype(BF16)
    return hi, lo


def _dot3(a, b):
    ah, al = _split(a)
    bh, bl = _split(b)
    return _dot(ah, bh) + _dot(al, bh) + _dot(ah, bl)


def _ada_kernel(c_ref, w_ref, b_ref, o_ref):
    cv = c_ref[...]
    s = cv * jax.nn.sigmoid(cv)
    o_ref[0] = _dot(s.astype(BF16), w_ref[0].astype(BF16)) + b_ref[0]


def _ada(cc, w_ada, b_ada):
    depth, d, d6 = w_ada.shape
    rows = cc.shape[0]
    tn = 1536
    return pl.pallas_call(
        _ada_kernel,
        grid=(depth, d6 // tn),
        in_specs=[
            pl.BlockSpec((rows, d), lambda l, j: (0, 0)),
            pl.BlockSpec((1, d, tn), lambda l, j: (l, 0, j)),
            pl.BlockSpec((1, 1, tn), lambda l, j: (l, 0, j)),
        ],
        out_specs=pl.BlockSpec((1, rows, tn), lambda l, j: (l, 0, j)),
        out_shape=jax.ShapeDtypeStruct((depth, rows, d6), F32),
        compiler_params=_cparams(("arbitrary", "arbitrary")),
        name="ada",
    )(cc, w_ada, b_ada.reshape(depth, 1, d6))


def _rot(x, cos, sa, sb):
    return x * cos + pltpu.roll(x, LANES - 16, 1) * sa + pltpu.roll(x, 16, 1) * sb


def _in_kernel(x_ref, sh_ref, sc_ref, g_ref, w_ref, *rest, rotary):
    if rotary:
        rc, rsa, rsb, ac, asa, asb, o16_ref, o32_ref = rest
    else:
        o16_ref, o32_ref = rest
    x = x_ref[0]
    h = (_rms(x) * g_ref[...]) * (1.0 + sc_ref[0]) + sh_ref[0]
    hb = h.astype(BF16)
    k_scale = RET_DK ** -0.5
    q_scale = ATT_HD ** -0.5 * LOG2E

    pr = _dot(hb, w_ref[:, 0:512])
    rq, rk, rv = pr[:, 0:128], pr[:, 128:256], pr[:, 256:512]
    if rotary:
        rq = _rot(rq, rc[...], rsa[...], rsb[...])
        rk = _rot(rk, rc[...], rsa[...], rsb[...])
    o16_ref[0, :, 0:128] = rq.astype(BF16)
    o16_ref[0, :, 128:256] = (rk * k_scale).astype(BF16)
    o16_ref[0, :, 256:512] = rv.astype(BF16)

    o32_ref[0] = _dot(hb, w_ref[:, 512:1536])

    pa = _dot(hb, w_ref[:, 1536:2304])
    for j in range(4):
        aq = pa[:, 128 * j:128 * (j + 1)]
        if rotary:
            aq = _rot(aq, ac[...], asa[...], asb[...])
        o16_ref[0, :, 512 + 128 * j:640 + 128 * j] = (aq * q_scale).astype(BF16)
    ak = pa[:, 512:640]
    if rotary:
        ak = _rot(ak, ac[...], asa[...], asb[...])
    o16_ref[0, :, 1024:1152] = ak.astype(BF16)
    o16_ref[0, :, 1152:1280] = pa[:, 640:768].astype(BF16)


def _in_proj(x3, mods, mod_row, g_pre, w_in16, tabs, tm):
    G, R, D = x3.shape
    rotary = tabs is not None
    nt = R // tm
    in_specs = [
        pl.BlockSpec((1, tm, D), lambda g, i: (g, i, 0)),
        pl.BlockSpec((1, 1, D), lambda g, i: (mod_row(g), 0, 0)),
        pl.BlockSpec((1, 1, D), lambda g, i: (mod_row(g), 0, 1)),
        _const_spec((1, D)),
        _const_spec(w_in16.shape),
    ]
    args = [x3, mods, mods, g_pre, w_in16]
    if rotary:
        in_specs += [pl.BlockSpec((tm, LANES), lambda g, i: (i, 0))] * 6
        args += list(tabs)
    return pl.pallas_call(
        functools.partial(_in_kernel, rotary=rotary),
        grid=(G, nt),
        in_specs=in_specs,
        out_specs=[
            pl.BlockSpec((1, tm, P16_W), lambda g, i: (g, i, 0)),
            pl.BlockSpec((1, tm, P32_W), lambda g, i: (g, i, 0)),
        ],
        out_shape=[
            jax.ShapeDtypeStruct((G, R, P16_W), BF16),
            jax.ShapeDtypeStruct((G, R, P32_W), F32),
        ],
        compiler_params=_cparams(("parallel", "parallel")),
        name="in_proj_rot" if rotary else "in_proj",
    )(*args)


def _ret_kernel(qkv_ref, rg_ref, dq_ref, ds_ref, dv_ref, g_ref, *rest, L, has_init):
    if has_init:
        s0_ref, o_ref, sfin_ref, st_scr = rest
    else:
        o_ref, sfin_ref, st_scr = rest
    C = RET_CHUNK
    N = L // C

    def log_gamma(ref, r):
        return jnp.log1p(-jnp.exp(ref[r:r + 1, :]))

    lfq, lbq = log_gamma(dq_ref, 0), log_gamma(dq_ref, 1)
    lfs, lbs = log_gamma(ds_ref, 0), log_gamma(ds_ref, 1)
    lfv, lbv = log_gamma(dv_ref, 0), log_gamma(dv_ref, 1)
    ri = lax.broadcasted_iota(jnp.int32, (C, LANES), 0).astype(F32)
    qdec_f = jnp.exp(lfq * (ri + 1.0))
    kdec_f = jnp.exp(lfq * (C - 1.0 - ri))
    qdec_b = jnp.exp(lbq * (C - ri))
    kdec_b = jnp.exp(lbq * ri)
    di = lax.broadcasted_iota(jnp.int32, (C, 4 * C), 0)
    dj = lax.broadcasted_iota(jnp.int32, (C, 4 * C), 1) & (C - 1)
    diff = (di - dj).astype(F32)
    dmask = (jnp.where(diff >= 0, jnp.exp(lfs * jnp.maximum(diff, 0.0)), 0.0)
             + jnp.where(diff <= 0, jnp.exp(lbs * jnp.maximum(-diff, 0.0)), 0.0))
    cdec_f = jnp.exp(lfv * float(C))
    cdec_b = jnp.exp(lbv * float(C))

    lane_q = lax.broadcasted_iota(jnp.int32, (1, RET_QK), 1) >> 5
    lane_v = lax.broadcasted_iota(jnp.int32, (1, RET_V), 1) >> 6
    hm = [jnp.where(lane_q == h, 1.0, 0.0).astype(BF16) for h in range(RET_HEADS)]
    cm = [jnp.where(lane_v == h, 1.0, 0.0).astype(BF16) for h in range(RET_HEADS)]
    bd = ((lax.broadcasted_iota(jnp.int32, (RET_QK, RET_V), 0) >> 5)
          == (lax.broadcasted_iota(jnp.int32, (RET_QK, RET_V), 1) >> 6))
    ones64 = jnp.where((lax.broadcasted_iota(jnp.int32, (RET_V, RET_V), 0) >> 6)
                       == (lax.broadcasted_iota(jnp.int32, (RET_V, RET_V), 1) >> 6), 1.0, 0.0).astype(BF16)

    if has_init:
        sf0 = s0_ref[0, 0]
        sb0 = s0_ref[0, 1]
    else:
        sf0 = jnp.zeros((RET_QK, RET_V), F32)
        sb0 = jnp.zeros((RET_QK, RET_V), F32)

    def kv_update(k16, v16, kdec, cdec, s):
        kd = (k16.astype(F32) * kdec).astype(BF16)
        return cdec * s + jnp.where(bd, _dot_tn(kd, v16), 0.0)

    def kv_at(n):
        r0 = pl.multiple_of(n * C, C)
        return qkv_ref[0, pl.ds(r0, C), 128:256], qkv_ref[0, pl.ds(r0, C), 256:512]

    U = min(RET_UNROLL, N)

    def scan_body(it, carry):
        sf, sb = carry
        for u in range(U):
            nf = it * U + u
            nb = N - 1 - nf
            st_scr[nf, 0:RET_QK, :] = sf.astype(BF16)
            sf = kv_update(*kv_at(nf), kdec_f, cdec_f, sf)
            st_scr[nb, RET_QK:2 * RET_QK, :] = sb.astype(BF16)
            sb = kv_update(*kv_at(nb), kdec_b, cdec_b, sb)
        return sf, sb

    sf_fin, sb_fin = lax.fori_loop(0, N // U, scan_body, (sf0, sb0))
    sfin_ref[0, 0] = sf_fin
    sfin_ref[0, 1] = sb_fin

    ones2 = jnp.concatenate([ones64, ones64], axis=0)

    def out_body(it, carry):
        ns = [it * U + u for u in range(U)]
        rows = [pl.ds(pl.multiple_of(n * C, C), C) for n in ns]
        q16s = [qkv_ref[0, r, 0:128] for r in rows]
        kvs = [kv_at(n) for n in ns]
        scores = [_dot_nt(q16, jnp.concatenate([k16 * hm[h] for h in range(RET_HEADS)], axis=0))
                  for q16, (k16, _) in zip(q16s, kvs)]
        outs = []
        for n, q16, (_, v16), s in zip(ns, q16s, kvs, scores):
            qf = q16.astype(F32)
            vblk = jnp.concatenate([v16 * cm[h] for h in range(RET_HEADS)], axis=0)
            lhs = jnp.concatenate([(s * dmask).astype(BF16), (qf * qdec_f).astype(BF16),
                                   (qf * qdec_b).astype(BF16)], axis=1)
            outs.append(_dot(lhs, jnp.concatenate([vblk, st_scr[n]], axis=0)))
        mss = []
        for o in outs:
            o2h, o2l = _split(o * o)
            mss.append(_dot(jnp.concatenate([o2h, o2l], axis=1), ones2) * (1.0 / RET_DV))
        for r, o, ms in zip(rows, outs, mss):
            rg = rg_ref[0, r, :]
            gated = (o * lax.rsqrt(ms + EPS)) * (rg * jax.nn.sigmoid(rg))
            o_ref[0, r, :] = (_rms(gated) * g_ref[...]).astype(BF16)
        return carry

    lax.fori_loop(0, N // U, out_body, 0)


def _retention(p16, p32, dq, ds, dv, g_ret, s0):
    B, L, _ = p16.shape
    has_init = s0 is not None
    in_specs = [
        pl.BlockSpec((1, L, 512), lambda b: (b, 0, 0)),
        pl.BlockSpec((1, L, RET_V), lambda b: (b, 0, 0)),
        _const_spec(dq.shape), _const_spec(ds.shape), _const_spec(dv.shape),
        _const_spec((1, RET_V)),
    ]
    args = [p16, p32, dq, ds, dv, g_ret]
    if has_init:
        in_specs.append(pl.BlockSpec((1, 2, RET_QK, RET_V), lambda b: (b, 0, 0, 0)))
        args.append(s0)
    return pl.pallas_call(
        functools.partial(_ret_kernel, L=L, has_init=has_init),
        grid=(B,),
        in_specs=in_specs,
        out_specs=[
            pl.BlockSpec((1, L, RET_V), lambda b: (b, 0, 0)),
            pl.BlockSpec((1, 2, RET_QK, RET_V), lambda b: (b, 0, 0, 0)),
        ],
        out_shape=[
            jax.ShapeDtypeStruct((B, L, RET_V), BF16),
            jax.ShapeDtypeStruct((B, 2, RET_QK, RET_V), F32),
        ],
        scratch_shapes=[pltpu.VMEM((L // RET_CHUNK, 2 * RET_QK, RET_V), BF16)],
        compiler_params=_cparams(("parallel",)),
        name="retention_init" if has_init else "retention",
    )(*args)


def _filt_kernel(zf_ref, w1_ref, b1_ref, w2_ref, b2_ref, w3_ref, fr_ref, dec_ref,
                 ch_ref, cl_ref, sh_ref, sl_ref, hr_ref, hi_ref, hn_ref,
                 ah_scr, al_scr, dh_scr, dl_scr, *, L, FC):
    i = pl.program_id(0)
    n = 2 * L

    @pl.when(i == 0)
    def _():
        fr = fr_ref[...]
        h = jnp.sin(fr * (_dot3(zf_ref[...], w1_ref[...]) + b1_ref[...]))
        for j in range(2):
            h = jnp.sin(fr * (_dot3(h, w2_ref[j]) + b2_ref[j]))
        h = _dot3(h, w3_ref[...])
        dec = dec_ref[...]
        row = lax.broadcasted_iota(jnp.int32, (L, HY_CH), 0)
        hf = h[:, 0:HY_CH] * dec
        hb = jnp.where(row == 0, 0.0, h[:, HY_CH:2 * HY_CH] * dec)
        a = hf + hb
        d = hf - hb
        sgn = jnp.where((row & 1) == 1, -1.0, 1.0)
        hn_ref[...] = jnp.sum(a * sgn, axis=0, keepdims=True) * (1.0 / n)
        ah, al = _split(a)
        dh, dl = _split(d)
        ah_scr[...] = ah
        al_scr[...] = al
        dh_scr[...] = dh
        dl_scr[...] = dl

    krow = lax.broadcasted_iota(jnp.int32, (FC, 1), 0) + i * FC
    wk = jnp.where(krow == 0, 1.0 / n, 2.0 / n)
    ch, cl, sh, sl = ch_ref[...], cl_ref[...], sh_ref[...], sl_ref[...]
    ah, al, dh, dl = ah_scr[...], al_scr[...], dh_scr[...], dl_scr[...]
    hr_ref[...] = (_dot(ch, ah) + _dot(ch, al) + _dot(cl, ah)) * wk
    hi_ref[...] = (_dot(sh, dh) + _dot(sh, dl) + _dot(sl, dh)) * wk


def _hyena_spectrum(L, zf, w1, b1, w2, b2, w3, fr, dec, mats):
    ch, cl, sh, sl = mats
    FC = min(512, L)
    mat_spec = pl.BlockSpec((FC, L), lambda i: (i, 0))
    out_spec = pl.BlockSpec((FC, HY_CH), lambda i: (i, 0))
    return pl.pallas_call(
        functools.partial(_filt_kernel, L=L, FC=FC),
        grid=(L // FC,),
        in_specs=[_const_spec(a.shape) for a in (zf, w1, b1, w2, b2, w3, fr, dec)] + [mat_spec] * 4,
        out_specs=[out_spec, out_spec, pl.BlockSpec((1, HY_CH), lambda i: (0, 0))],
        out_shape=[
            jax.ShapeDtypeStruct((L, HY_CH), F32),
            jax.ShapeDtypeStruct((L, HY_CH), F32),
            jax.ShapeDtypeStruct((1, HY_CH), F32),
        ],
        scratch_shapes=[pltpu.VMEM((L, HY_CH), BF16)] * 4,
        compiler_params=_cparams(("arbitrary",)),
        name="hyena_filter",
    )(zf, w1, b1, w2, b2, w3, fr, dec, ch, cl, sh, sl)


def _hy_kernel(v_ref, x1_ref, x0_ref, sw_ref, sb_ref, bias_ref, g_ref, c_ref, s_ref,
               hr_ref, hi_ref, hn_ref, o_ref, z_scr, x0_scr, zb_scr, yr_scr, yi_scr, *, L, R):
    nchunks = L // R
    zn = jnp.zeros((1, HY_CH), F32)
    for c in range(nchunks):
        r0 = c * R
        lo = max(r0 - 8, 0)
        hi = min(r0 + R + 8, L)
        rows = hi - lo
        off = r0 - lo
        grow = lax.broadcasted_iota(jnp.int32, (R, HY_CH), 0) + r0

        def conv(ref, c0):
            ext = ref[0, lo:hi, :]
            up = jnp.where(grow == 0, 0.0, pltpu.roll(ext, 1, 0)[off:off + R])
            un = jnp.where(grow == L - 1, 0.0, pltpu.roll(ext, rows - 1, 0)[off:off + R])
            u = ref[0, r0:r0 + R, :]
            return (up * sw_ref[0:1, c0:c0 + HY_CH] + u * sw_ref[1:2, c0:c0 + HY_CH]
                    + un * sw_ref[2:3, c0:c0 + HY_CH] + sb_ref[:, c0:c0 + HY_CH])

        z = conv(v_ref, 0) * conv(x1_ref, HY_CH)
        z_scr[r0:r0 + R, :] = z
        zb_scr[r0:r0 + R, :] = z.astype(BF16)
        x0_scr[r0:r0 + R, :] = conv(x0_ref, 2 * HY_CH)
        sgn = jnp.where((grow & 1) == 1, -1.0, 1.0)
        zn = zn + jnp.sum(z * sgn, axis=0, keepdims=True)

    zb = zb_scr[...]
    for c in range(nchunks):
        r0 = c * R
        zr = _dot(c_ref[r0:r0 + R, :], zb)
        zi = _dot(s_ref[r0:r0 + R, :], zb)
        hr = hr_ref[r0:r0 + R, :]
        hi_ = hi_ref[r0:r0 + R, :]
        yr_scr[r0:r0 + R, :] = (zr * hr - zi * hi_).astype(BF16)
        yi_scr[r0:r0 + R, :] = (zr * hi_ + zi * hr).astype(BF16)

    yr = yr_scr[...]
    yi = yi_scr[...]
    nyq = zn * hn_ref[...]
    for c in range(nchunks):
        r0 = c * R
        y = _dot(c_ref[r0:r0 + R, :], yr) + _dot(s_ref[r0:r0 + R, :], yi)
        trow = lax.broadcasted_iota(jnp.int32, (R, HY_CH), 0) + r0
        y = y + jnp.where((trow & 1) == 1, -nyq, nyq)
        y = y + z_scr[r0:r0 + R, :] * bias_ref[...]
        out = y * x0_scr[r0:r0 + R, :]
        o_ref[0, r0:r0 + R, :] = (_rms(out) * g_ref[...]).astype(BF16)


def _hyena(p32, short_w, short_b, hy_bias, g_hy, cmat, smat, hr, hi, hn):
    B, L, _ = p32.shape
    R = min(256, L)
    col = lambda j: pl.BlockSpec((1, L, HY_CH), lambda b: (b, 0, j))
    return pl.pallas_call(
        functools.partial(_hy_kernel, L=L, R=R),
        grid=(B,),
        in_specs=[col(1), col(2), col(3)] + [
            _const_spec(a.shape) for a in (short_w, short_b, hy_bias, g_hy, cmat, smat, hr, hi, hn)],
        out_specs=pl.BlockSpec((1, L, HY_CH), lambda b: (b, 0, 0)),
        out_shape=jax.ShapeDtypeStruct((B, L, HY_CH), BF16),
        scratch_shapes=[
            pltpu.VMEM((L, HY_CH), F32), pltpu.VMEM((L, HY_CH), F32),
            pltpu.VMEM((L, HY_CH), BF16), pltpu.VMEM((L, HY_CH), BF16), pltpu.VMEM((L, HY_CH), BF16),
        ],
        compiler_params=_cparams(("parallel",)),
        name="hyena",
    )(p32, p32, p32, short_w, short_b, hy_bias, g_hy, cmat, smat, hr, hi, hn)


def _expand_kv(kv_ref, k2_scr, vv_scr, rows):
    R = min(256, rows)
    lo = lax.broadcasted_iota(jnp.int32, (R, LANES), 1) < ATT_HD
    ones = jnp.ones((R, LANES), BF16)
    for c in range(rows // R):
        sl = slice(c * R, (c + 1) * R)
        k = kv_ref[0, sl, 0:128].astype(F32)
        v = kv_ref[0, sl, 128:256].astype(F32)
        kr = pltpu.roll(k, ATT_HD, 1)
        vr = pltpu.roll(v, ATT_HD, 1)
        k2_scr[0, sl, :] = jnp.where(lo, k, kr).astype(BF16)
        k2_scr[1, sl, :] = jnp.where(lo, kr, k).astype(BF16)
        vv_scr[0, sl, 0:128] = jnp.where(lo, v, vr).astype(BF16)
        vv_scr[1, sl, 0:128] = jnp.where(lo, vr, v).astype(BF16)
        vv_scr[0, sl, 128:256] = ones
        vv_scr[1, sl, 128:256] = ones


def _att_kernel(sink_ref, q_ref, *rest, L, Lc, band):
    if band:
        (kv_ref, ckv_ref, g_ref, o_ref, k2_scr, vv_scr, ck2_scr, cvv_scr, att_scr) = rest
        _expand_kv(kv_ref, k2_scr, vv_scr, L)
    else:
        (ckv_ref, g_ref, o_ref, ck2_scr, cvv_scr, att_scr) = rest
    _expand_kv(ckv_ref, ck2_scr, cvv_scr, Lc)

    T = ATT_BLOCK
    nb = L // T
    lane = lax.broadcasted_iota(jnp.int32, (1, LANES), 1)
    lo = lane < ATT_HD
    lo16 = jnp.where(lo, 1.0, 0.0).astype(BF16)
    hi16 = jnp.where(lo, 0.0, 1.0).astype(BF16)
    hrow = lax.broadcasted_iota(jnp.int32, (4 * T, 1), 0) >> 7
    ti = lax.broadcasted_iota(jnp.int32, (T, T), 0)
    tj = lax.broadcasted_iota(jnp.int32, (T, T), 1)
    tri_prev = jnp.where(tj >= ti, 0.0, NEG_INF)
    tri_next = jnp.where(tj <= ti, 0.0, NEG_INF)

    def add_bias(s, bias):
        return (s.reshape(4, T, T) + bias[None]).reshape(4 * T, T)

    def block(n, carry):
        r0 = pl.multiple_of(n * T, T)
        qb = q_ref[0, pl.ds(r0, T), :]
        if band:
            rp = pl.multiple_of(jnp.maximum(n - 1, 0) * T, T)
            rn = pl.multiple_of(jnp.minimum(n + 1, nb - 1) * T, T)
            bias_p = tri_prev + jnp.where(n == 0, NEG_INF, 0.0)
            bias_n = tri_next + jnp.where(n == nb - 1, NEG_INF, 0.0)
        groups = range(ATT_KV_HEADS)

        def keys_of(scr, cscr, g):
            parts = [cscr[g]]
            if band:
                parts += [scr[g, pl.ds(rp, T), :], scr[g, pl.ds(r0, T), :], scr[g, pl.ds(rn, T), :]]
            return jnp.concatenate(parts, axis=0)

        scores, sinks = [], []
        for g in groups:
            qs = []
            for p in range(2):
                qp = qb[:, 128 * (2 * g + p):128 * (2 * g + p + 1)]
                qs += [qp * lo16, qp * hi16]
            qst = jnp.concatenate(qs, axis=0)
            scores.append(_dot_nt(qst, keys_of(k2_scr if band else None, ck2_scr, g)))
            sinks.append(LOG2E * jnp.where(
                hrow == 0, sink_ref[4 * g],
                jnp.where(hrow == 1, sink_ref[4 * g + 1],
                          jnp.where(hrow == 2, sink_ref[4 * g + 2], sink_ref[4 * g + 3]))))
        probs, esks = [], []
        for s, sk in zip(scores, sinks):
            cols = [s[:, j * T:(j + 1) * T] for j in range(s.shape[1] // T)]
            if band:
                cols[-3] = add_bias(cols[-3], bias_p)
                cols[-1] = add_bias(cols[-1], bias_n)
            mx = cols[0]
            for c_ in cols[1:]:
                mx = jnp.maximum(mx, c_)
            m = jnp.maximum(jnp.max(mx, axis=-1, keepdims=True), sk)
            probs.append(jnp.concatenate([jnp.exp2(c_ - m).astype(BF16) for c_ in cols], axis=1))
            esks.append(jnp.exp2(sk - m))
        for g, pr, esk in zip(groups, probs, esks):
            pv = _dot(pr, keys_of(vv_scr if band else None, cvv_scr, g))
            res = pv[:, 0:LANES] / (pv[:, LANES:2 * LANES] + esk)
            for p in range(2):
                c0 = 128 * (2 * g + p)
                att_scr[:, c0:c0 + 128] = jnp.where(lo, res[2 * p * T:(2 * p + 1) * T],
                                                    res[(2 * p + 1) * T:(2 * p + 2) * T])
        o_ref[0, pl.ds(r0, T), :] = (_rms(att_scr[...]) * g_ref[...]).astype(BF16)
        return carry

    lax.fori_loop(0, nb, block, 0)


def _attention(sink, p16, pc16, g_att, band):
    B, Lc, _ = pc16.shape
    L = p16.shape[1] if band else Lc
    qsrc = p16 if band else pc16
    kv_spec = lambda n: pl.BlockSpec((1, n, 256), lambda b: (b, 0, 4))
    in_specs = [pl.BlockSpec(memory_space=pltpu.SMEM),
                pl.BlockSpec((1, L, ATT_Q), lambda b: (b, 0, 1))]
    args = [sink, qsrc]
    scratch = []
    if band:
        in_specs.append(kv_spec(L))
        args.append(p16)
        scratch += [pltpu.VMEM((2, L, LANES), BF16), pltpu.VMEM((2, L, 2 * LANES), BF16)]
    in_specs += [kv_spec(Lc), _const_spec((1, ATT_Q))]
    args += [pc16, g_att]
    scratch += [pltpu.VMEM((2, Lc, LANES), BF16), pltpu.VMEM((2, Lc, 2 * LANES), BF16),
                pltpu.VMEM((ATT_BLOCK, ATT_Q), F32)]
    return pl.pallas_call(
        functools.partial(_att_kernel, L=L, Lc=Lc, band=band),
        grid=(B,),
        in_specs=in_specs,
        out_specs=pl.BlockSpec((1, L, ATT_Q), lambda b: (b, 0, 0)),
        out_shape=jax.ShapeDtypeStruct((B, L, ATT_Q), BF16),
        scratch_shapes=scratch,
        compiler_params=_cparams(("parallel",)),
        name="attn_window" if band else "attn_ctx",
    )(*args)


def _out_mlp_kernel(x_ref, nr_ref, nh_ref, na_ref, gta_ref, shm_ref, scm_ref, gtm_ref,
                    gpm_ref, gpre_ref, gpost_ref, wo_ref, w1_ref, w2_ref, o_ref, *, ff_chunk):
    x = x_ref[0]
    mix = (_dot(nr_ref[0], wo_ref[0:256, :]) + _dot(nh_ref[0], wo_ref[256:512, :])
           + _dot(na_ref[0], wo_ref[512:1024, :]))
    x1 = x + gta_ref[0] * (_rms(mix) * gpm_ref[...])
    h = ((_rms(x1) * gpre_ref[...]) * (1.0 + scm_ref[0]) + shm_ref[0]).astype(BF16)
    dff = w1_ref.shape[1]
    acc = None
    for j in range(dff // ff_chunk):
        sl = slice(j * ff_chunk, (j + 1) * ff_chunk)
        hj = jnp.square(jnp.maximum(_dot(h, w1_ref[:, sl]), 0.0)).astype(BF16)
        part = _dot(hj, w2_ref[sl, :])
        acc = part if acc is None else acc + part
    o_ref[0] = x1 + gtm_ref[0] * (_rms(acc) * gpost_ref[...])


def _out_mlp(x3, nr, nh, na, mods, mod_row, g_post_mix, g_pre_mlp, g_post_mlp, wo16, w116, w216, tm):
    G, R, D = x3.shape
    row = lambda w: pl.BlockSpec((1, tm, w), lambda g, i: (g, i, 0))
    mod = lambda j: pl.BlockSpec((1, 1, D), lambda g, i: (mod_row(g), 0, j))
    return pl.pallas_call(
        functools.partial(_out_mlp_kernel, ff_chunk=1024),
        grid=(G, R // tm),
        in_specs=[row(D), row(RET_V), row(HY_CH), row(ATT_Q), mod(2), mod(3), mod(4), mod(5),
                  _const_spec((1, D)), _const_spec((1, D)), _const_spec((1, D)),
                  _const_spec(wo16.shape), _const_spec(w116.shape), _const_spec(w216.shape)],
        out_specs=row(D),
        out_shape=jax.ShapeDtypeStruct((G, R, D), F32),
        compiler_params=_cparams(("parallel", "parallel")),
        name="out_mlp",
    )(x3, nr, nh, na, mods, mods, mods, mods, g_post_mix, g_pre_mlp, g_post_mlp, wo16, w116, w216)


def _rot_tables(cos, sin):
    lane = jnp.arange(LANES) % 32
    first = (lane < 16)[None, :]
    return cos, jnp.where(first, -sin, 0.0), jnp.where(first, 0.0, sin)


def _rotary_tables(L):
    pos = jnp.arange(L)
    row = (pos // GRID_W).astype(F32)
    col = (pos % GRID_W).astype(F32)
    half = ATT_HD // 2
    inv_ax = ROPE_BASE ** (-jnp.arange(0, half, 2, dtype=F32) / half)

    def tab(p, inv):
        ang = p[:, None] * inv[None, :]
        ang = jnp.concatenate([ang, ang], axis=-1)
        return jnp.cos(ang), jnp.sin(ang)

    cr, sr = tab(row, inv_ax)
    cc, sc = tab(col, inv_ax)
    inv_ret = 1.0 / (RET_ROT_BASE ** jnp.linspace(0.0, 1.0, RET_DK // 2, dtype=F32))
    ct, st = tab(pos.astype(F32), inv_ret)
    ret = _rot_tables(jnp.tile(ct, (1, 4)), jnp.tile(st, (1, 4)))
    ax = _rot_tables(jnp.tile(jnp.concatenate([cr, cc], -1), (1, 2)),
                     jnp.tile(jnp.concatenate([sr, sc], -1), (1, 2)))
    return ret + ax


def _dft_mats(L):
    n = 2 * L
    k = jnp.arange(L, dtype=jnp.int32)
    ang = ((k[:, None] * k[None, :]) % n).astype(F32) * (2.0 * math.pi / n)
    c = jnp.cos(ang)
    s = -jnp.sin(ang)
    ch, sh = c.astype(BF16), s.astype(BF16)
    cl = (c - ch.astype(F32)).astype(BF16)
    sl = (s - sh.astype(F32)).astype(BF16)
    return ch, cl, sh, sl


def _filter_features(L):
    t = jnp.linspace(0.0, 1.0, L, dtype=F32)[:, None]
    w = 2.0 * math.pi * jnp.arange(L, dtype=F32) / L
    bands = jnp.linspace(1e-4, HY_BANDS - 1, HY_BANDS, dtype=F32)
    ang = w[:, None] * bands[None, :]
    z = jnp.concatenate([t, jnp.cos(ang), -jnp.sin(ang)], axis=-1)
    zf = jnp.pad(z, ((0, 0), (0, LANES - HY_EMB)))
    max_decay = math.log(HY_TARGET) / HY_FAST_DECAY
    min_decay = math.log(HY_TARGET) / HY_SLOW_DECAY
    deltas = jnp.linspace(min_decay, max_decay, HY_CH, dtype=F32)
    dec = jnp.exp(-t * jnp.abs(deltas)[None, :])
    return zf, dec


def _pad_to(a, shape):
    return jnp.pad(a, [(0, s - d) for d, s in zip(a.shape, shape)])


def kernel(x, c, ctx, c_ctx, w_ada, b_ada, g_pre_mix, g_post_mix, g_pre_mlp, g_post_mlp, w_in,
           ret_decay_fwd, ret_decay_bwd, hy_short_w, hy_short_b, hy_f_w1, hy_f_b1, hy_f_w2, hy_f_b2,
           hy_f_w3, hy_f_freq, hy_bias, attn_sink, g_ret, g_hy, g_att, w_out, w_ff1, w_ff2):
    B, L, D = x.shape
    Lc = ctx.shape[1]
    depth = w_ada.shape[0]
    assert D == 1024 and w_in.shape[2] == 2304 and L % 256 == 0 and Lc % 256 == 0 and L >= 3 * ATT_BLOCK

    rows = -(-(B + 1) // 8) * 8
    cc = _pad_to(jnp.concatenate([c, c_ctx[None, :]], axis=0), (rows, D))
    mods_all = _ada(cc, w_ada, b_ada).reshape(depth, rows, 1, 6 * D)

    w_in16 = w_in.astype(BF16)
    w_out16 = w_out.astype(BF16)
    w_ff116 = w_ff1.astype(BF16)
    w_ff216 = w_ff2.astype(BF16)

    tabs = _rotary_tables(L)
    mats = {n: _dft_mats(n) for n in {L, Lc}}
    feats = {n: _filter_features(n) for n in {L, Lc}}

    tm = 512
    ctx_rows = B * Lc
    xc = ctx.reshape(ctx_rows // tm, tm, D)
    lat_row = lambda g: g
    ctx_row = lambda g: B

    for l in range(depth):
        last = l == depth - 1
        mods = mods_all[l]
        dec2 = jnp.stack([ret_decay_fwd[l], ret_decay_bwd[l]])
        dq = jnp.repeat(dec2, RET_DK, axis=1)
        ds = jnp.repeat(dec2, RET_CHUNK, axis=1)
        dv = jnp.repeat(dec2, RET_DV, axis=1)
        gr, gh, ga = g_ret[l][None], g_hy[l][None], g_att[l][None]
        gpre, gpm = g_pre_mix[l][None], g_post_mix[l][None]
        gprm, gpom = g_pre_mlp[l][None], g_post_mlp[l][None]
        fw1 = _pad_to(hy_f_w1[l], (LANES, LANES))
        fb1 = _pad_to(hy_f_b1[l][None], (1, LANES))
        fw2 = _pad_to(hy_f_w2[l], (2, LANES, LANES))
        fb2 = _pad_to(hy_f_b2[l][:, None, :], (2, 1, LANES))
        fw3 = _pad_to(hy_f_w3[l], (LANES, 2 * HY_CH))
        ffr = _pad_to(hy_f_freq[l][None], (1, LANES))
        sw, sb, hbias = hy_short_w[l], hy_short_b[l][None], hy_bias[l][None]

        def spectrum(n):
            zf, dec = feats[n]
            return _hyena_spectrum(n, zf, fw1, fb1, fw2, fb2, fw3, ffr, dec, mats[n])

        pc16, pc32 = _in_proj(xc, mods, ctx_row, gpre, w_in16[l], None, tm)
        pc16 = pc16.reshape(B, Lc, P16_W)
        pc32 = pc32.reshape(B, Lc, P32_W)
        cret, cstate = _retention(pc16, pc32, dq, ds, dv, gr, None)

        p16, p32 = _in_proj(x, mods, lat_row, gpre, w_in16[l], tabs, tm)
        ret, _ = _retention(p16, p32, dq, ds, dv, gr, cstate)
        hr, hi, hn = spectrum(L)
        hyo = _hyena(p32, sw, sb, hbias, gh, mats[L][0], mats[L][2], hr, hi, hn)
        att = _attention(attn_sink[l], p16, pc16, ga, True)
        x = _out_mlp(x, ret, hyo, att, mods, lat_row, gpm, gprm, gpom,
                     w_out16[l], w_ff116[l], w_ff216[l], tm)

        if not last:
            chr_, chi, chn = spectrum(Lc)
            chyo = _hyena(pc32, sw, sb, hbias, gh, mats[Lc][0], mats[Lc][2], chr_, chi, chn)
            catt = _attention(attn_sink[l], None, pc16, ga, False)
            r3 = lambda a: a.reshape(ctx_rows // tm, tm, a.shape[-1])
            xc = _out_mlp(xc, r3(cret), r3(chyo), r3(catt), mods, ctx_row, gpm, gprm, gpom,
                          w_out16[l], w_ff116[l], w_ff216[l], tm)
    return x
```

```python
import functools
import math

import jax
import jax.numpy as jnp
from jax import lax
from jax.experimental import pallas as pl
from jax.experimental.pallas import tpu as pltpu

F32 = jnp.float32
BF16 = jnp.bfloat16
EPS = 1e-6
NEG_INF = -1e30
LOG2E = math.log2(math.e)

RET_HEADS = 4
RET_DK = 32
RET_DV = 64
RET_QK = RET_HEADS * RET_DK
RET_V = RET_HEADS * RET_DV
RET_CHUNK = 128
RET_UNROLL = 4
RET_ROT_BASE = 10000.0
HY_CH = 256
HY_EMB = 33
HY_BANDS = 16
HY_ORDER = 64
HY_FAST_DECAY = 0.3
HY_SLOW_DECAY = 1.5
HY_TARGET = 1e-2
ATT_HEADS = 8
ATT_KV_HEADS = 2
ATT_HD = 64
ATT_Q = ATT_HEADS * ATT_HD
ATT_KV = ATT_KV_HEADS * ATT_HD
WINDOW = 128
ATT_BLOCK = 128
ATT_UNROLL = 4
ROPE_BASE = 10000.0
GRID_W = 64

P16_W = 2 * RET_QK + RET_V + ATT_Q + 2 * ATT_KV
P32_W = RET_V + 3 * HY_CH

LANES = 128
VMEM_LIMIT = 56 * 1024 * 1024


def _cparams(sem):
    return pltpu.CompilerParams(dimension_semantics=sem, vmem_limit_bytes=VMEM_LIMIT)


def _const_spec(shape):
    nd = len(shape)
    return pl.BlockSpec(shape, lambda *_: (0,) * nd, pipeline_mode=pl.Buffered(1))


def _rms(x):
    return x * lax.rsqrt(jnp.mean(x * x, axis=-1, keepdims=True) + EPS)


def _dot(a, b):
    return jnp.dot(a, b, preferred_element_type=F32)


def _dot_nt(a, b):
    return lax.dot_general(a, b, (((1,), (1,)), ((), ())), preferred_element_type=F32)


def _dot_tn(a, b):
    return lax.dot_general(a, b, (((0,), (0,)), ((), ())), preferred_element_type=F32)


def _split(a):
    hi = a.astype(BF16)
    lo = (a - hi.astype(F32)).astype(BF16)
    return hi, lo


def _dot3(a, b):
    ah, al = _split(a)
    bh, bl = _split(b)
    return _dot(ah, bh) + _dot(al, bh) + _dot(ah, bl)


def _ada_kernel(c_ref, w_ref, b_ref, o_ref):
    cv = c_ref[...]
    s = cv * jax.nn.sigmoid(cv)
    o_ref[0] = _dot(s.astype(BF16), w_ref[0].astype(BF16)) + b_ref[0]


def _ada(cc, w_ada, b_ada):
    depth, d, d6 = w_ada.shape
    rows = cc.shape[0]
    tn = 1536
    return pl.pallas_call(
        _ada_kernel,
        grid=(depth, d6 // tn),
        in_specs=[
            pl.BlockSpec((rows, d), lambda l, j: (0, 0)),
            pl.BlockSpec((1, d, tn), lambda l, j: (l, 0, j)),
            pl.BlockSpec((1, 1, tn), lambda l, j: (l, 0, j)),
        ],
        out_specs=pl.BlockSpec((1, rows, tn), lambda l, j: (l, 0, j)),
        out_shape=jax.ShapeDtypeStruct((depth, rows, d6), F32),
        compiler_params=_cparams(("arbitrary", "arbitrary")),
        name="ada",
    )(cc, w_ada, b_ada.reshape(depth, 1, d6))


def _rot(x, cos, sa, sb):
    return x * cos + pltpu.roll(x, LANES - 16, 1) * sa + pltpu.roll(x, 16, 1) * sb


def _in_kernel(x_ref, sh_ref, sc_ref, g_ref, w_ref, *rest, rotary, sub):
    if rotary:
        rc, rsa, rsb, ac, asa, asb, o16_ref, o32_ref = rest
    else:
        o16_ref, o32_ref = rest
    tm = x_ref.shape[1]
    k_scale = RET_DK ** -0.5
    q_scale = ATT_HD ** -0.5 * LOG2E

    def normed(i):
        x = x_ref[0, i * sub:(i + 1) * sub, :]
        return ((_rms(x) * g_ref[...]) * (1.0 + sc_ref[0]) + sh_ref[0]).astype(BF16)

    def project(i, hb):
        r = slice(i * sub, (i + 1) * sub)

        def rot(v, tabs):
            return _rot(v, *[t[r, :] for t in tabs]) if rotary else v

        ret_tabs = (rc, rsa, rsb) if rotary else None
        ax_tabs = (ac, asa, asb) if rotary else None
        pr = _dot(hb, w_ref[:, 0:512])
        o16_ref[0, r, 0:128] = rot(pr[:, 0:128], ret_tabs).astype(BF16)
        o16_ref[0, r, 128:256] = (rot(pr[:, 128:256], ret_tabs) * k_scale).astype(BF16)
        o16_ref[0, r, 256:512] = pr[:, 256:512].astype(BF16)
        o32_ref[0, r, :] = _dot(hb, w_ref[:, 512:1536])
        pa = _dot(hb, w_ref[:, 1536:2304])
        for j in range(4):
            aq = rot(pa[:, 128 * j:128 * (j + 1)], ax_tabs)
            o16_ref[0, r, 512 + 128 * j:640 + 128 * j] = (aq * q_scale).astype(BF16)
        o16_ref[0, r, 1024:1152] = rot(pa[:, 512:640], ax_tabs).astype(BF16)
        o16_ref[0, r, 1152:1280] = pa[:, 640:768].astype(BF16)

    nxt = normed(0)
    for i in range(tm // sub):
        cur = nxt
        if i + 1 < tm // sub:
            nxt = normed(i + 1)
        project(i, cur)


def _in_proj(x3, mods, mod_row, g_pre, w_in16, tabs, tm):
    G, R, D = x3.shape
    rotary = tabs is not None
    nt = R // tm
    in_specs = [
        pl.BlockSpec((1, tm, D), lambda g, i: (g, i, 0)),
        pl.BlockSpec((1, 1, D), lambda g, i: (mod_row(g), 0, 0)),
        pl.BlockSpec((1, 1, D), lambda g, i: (mod_row(g), 0, 1)),
        _const_spec((1, D)),
        _const_spec(w_in16.shape),
    ]
    args = [x3, mods, mods, g_pre, w_in16]
    if rotary:
        in_specs += [pl.BlockSpec((tm, LANES), lambda g, i: (i, 0))] * 6
        args += list(tabs)
    return pl.pallas_call(
        functools.partial(_in_kernel, rotary=rotary, sub=min(256, tm)),
        grid=(G, nt),
        in_specs=in_specs,
        out_specs=[
            pl.BlockSpec((1, tm, P16_W), lambda g, i: (g, i, 0)),
            pl.BlockSpec((1, tm, P32_W), lambda g, i: (g, i, 0)),
        ],
        out_shape=[
            jax.ShapeDtypeStruct((G, R, P16_W), BF16),
            jax.ShapeDtypeStruct((G, R, P32_W), F32),
        ],
        compiler_params=_cparams(("parallel", "parallel")),
        name="in_proj_rot" if rotary else "in_proj",
    )(*args)


def _ret_kernel(qkv_ref, rg_ref, dq_ref, ds_ref, dv_ref, g_ref, *rest, L, has_init):
    if has_init:
        s0_ref, o_ref, sfin_ref, st_scr = rest
    else:
        o_ref, sfin_ref, st_scr = rest
    C = RET_CHUNK
    N = L // C

    def log_gamma(ref, r):
        return jnp.log1p(-jnp.exp(ref[r:r + 1, :]))

    lfq, lbq = log_gamma(dq_ref, 0), log_gamma(dq_ref, 1)
    lfs, lbs = log_gamma(ds_ref, 0), log_gamma(ds_ref, 1)
    lfv, lbv = log_gamma(dv_ref, 0), log_gamma(dv_ref, 1)
    ri = lax.broadcasted_iota(jnp.int32, (C, LANES), 0).astype(F32)
    qdec_f = jnp.exp(lfq * (ri + 1.0))
    kdec_f = jnp.exp(lfq * (C - 1.0 - ri))
    qdec_b = jnp.exp(lbq * (C - ri))
    kdec_b = jnp.exp(lbq * ri)
    di = lax.broadcasted_iota(jnp.int32, (C, 4 * C), 0)
    dj = lax.broadcasted_iota(jnp.int32, (C, 4 * C), 1) & (C - 1)
    diff = (di - dj).astype(F32)
    dmask = (jnp.where(diff >= 0, jnp.exp(lfs * jnp.maximum(diff, 0.0)), 0.0)
             + jnp.where(diff <= 0, jnp.exp(lbs * jnp.maximum(-diff, 0.0)), 0.0))
    cdec_f = jnp.exp(lfv * float(C))
    cdec_b = jnp.exp(lbv * float(C))

    lane_q = lax.broadcasted_iota(jnp.int32, (1, RET_QK), 1) >> 5
    lane_v = lax.broadcasted_iota(jnp.int32, (1, RET_V), 1) >> 6
    hm = [jnp.where(lane_q == h, 1.0, 0.0).astype(BF16) for h in range(RET_HEADS)]
    cm = [jnp.where(lane_v == h, 1.0, 0.0).astype(BF16) for h in range(RET_HEADS)]
    bd = ((lax.broadcasted_iota(jnp.int32, (RET_QK, RET_V), 0) >> 5)
          == (lax.broadcasted_iota(jnp.int32, (RET_QK, RET_V), 1) >> 6))
    ones64 = jnp.where((lax.broadcasted_iota(jnp.int32, (RET_V, RET_V), 0) >> 6)
                       == (lax.broadcasted_iota(jnp.int32, (RET_V, RET_V), 1) >> 6), 1.0, 0.0).astype(BF16)

    if has_init:
        sf0 = s0_ref[0, 0]
        sb0 = s0_ref[0, 1]
    else:
        sf0 = jnp.zeros((RET_QK, RET_V), F32)
        sb0 = jnp.zeros((RET_QK, RET_V), F32)

    def kv_update(k16, v16, kdec, cdec, s):
        kd = (k16.astype(F32) * kdec).astype(BF16)
        return cdec * s + jnp.where(bd, _dot_tn(kd, v16), 0.0)

    def kv_at(n):
        r0 = pl.multiple_of(n * C, C)
        return qkv_ref[0, pl.ds(r0, C), 128:256], qkv_ref[0, pl.ds(r0, C), 256:512]

    U = min(RET_UNROLL, N)

    def scan_body(it, carry):
        sf, sb = carry
        for u in range(U):
            nf = it * U + u
            nb = N - 1 - nf
            st_scr[nf, 0:RET_QK, :] = sf.astype(BF16)
            sf = kv_update(*kv_at(nf), kdec_f, cdec_f, sf)
            st_scr[nb, RET_QK:2 * RET_QK, :] = sb.astype(BF16)
            sb = kv_update(*kv_at(nb), kdec_b, cdec_b, sb)
        return sf, sb

    sf_fin, sb_fin = lax.fori_loop(0, N // U, scan_body, (sf0, sb0))
    sfin_ref[0, 0] = sf_fin
    sfin_ref[0, 1] = sb_fin

    ones2 = jnp.concatenate([ones64, ones64], axis=0)

    def out_body(it, carry):
        ns = [it * U + u for u in range(U)]
        rows = [pl.ds(pl.multiple_of(n * C, C), C) for n in ns]
        q16s = [qkv_ref[0, r, 0:128] for r in rows]
        kvs = [kv_at(n) for n in ns]
        scores = [_dot_nt(q16, jnp.concatenate([k16 * hm[h] for h in range(RET_HEADS)], axis=0))
                  for q16, (k16, _) in zip(q16s, kvs)]
        outs = []
        for n, q16, (_, v16), s in zip(ns, q16s, kvs, scores):
            qf = q16.astype(F32)
            vblk = jnp.concatenate([v16 * cm[h] for h in range(RET_HEADS)], axis=0)
            lhs = jnp.concatenate([(s * dmask).astype(BF16), (qf * qdec_f).astype(BF16),
                                   (qf * qdec_b).astype(BF16)], axis=1)
            outs.append(_dot(lhs, jnp.concatenate([vblk, st_scr[n]], axis=0)))
        mss = []
        for o in outs:
            o2h, o2l = _split(o * o)
            mss.append(_dot(jnp.concatenate([o2h, o2l], axis=1), ones2) * (1.0 / RET_DV))
        for r, o, ms in zip(rows, outs, mss):
            rg = rg_ref[0, r, :]
            gated = (o * lax.rsqrt(ms + EPS)) * (rg * jax.nn.sigmoid(rg))
            o_ref[0, r, :] = (_rms(gated) * g_ref[...]).astype(BF16)
        return carry

    lax.fori_loop(0, N // U, out_body, 0)


def _retention(p16, p32, dq, ds, dv, g_ret, s0):
    B, L, _ = p16.shape
    has_init = s0 is not None
    in_specs = [
        pl.BlockSpec((1, L, 512), lambda b: (b, 0, 0)),
        pl.BlockSpec((1, L, RET_V), lambda b: (b, 0, 0)),
        _const_spec(dq.shape), _const_spec(ds.shape), _const_spec(dv.shape),
        _const_spec((1, RET_V)),
    ]
    args = [p16, p32, dq, ds, dv, g_ret]
    if has_init:
        in_specs.append(pl.BlockSpec((1, 2, RET_QK, RET_V), lambda b: (b, 0, 0, 0)))
        args.append(s0)
    return pl.pallas_call(
        functools.partial(_ret_kernel, L=L, has_init=has_init),
        grid=(B,),
        in_specs=in_specs,
        out_specs=[
            pl.BlockSpec((1, L, RET_V), lambda b: (b, 0, 0)),
            pl.BlockSpec((1, 2, RET_QK, RET_V), lambda b: (b, 0, 0, 0)),
        ],
        out_shape=[
            jax.ShapeDtypeStruct((B, L, RET_V), BF16),
            jax.ShapeDtypeStruct((B, 2, RET_QK, RET_V), F32),
        ],
        scratch_shapes=[pltpu.VMEM((L // RET_CHUNK, 2 * RET_QK, RET_V), BF16)],
        compiler_params=_cparams(("parallel",)),
        name="retention_init" if has_init else "retention",
    )(*args)


def _filt_kernel(zf_ref, w1_ref, b1_ref, w2_ref, b2_ref, w3_ref, fr_ref, dec_ref,
                 ch_ref, cl_ref, sh_ref, sl_ref, hr_ref, hi_ref, hn_ref,
                 ah_scr, al_scr, dh_scr, dl_scr, *, L, FC):
    i = pl.program_id(0)
    n = 2 * L

    @pl.when(i == 0)
    def _():
        fr = fr_ref[...]
        h = jnp.sin(fr * (_dot3(zf_ref[...], w1_ref[...]) + b1_ref[...]))
        for j in range(2):
            h = jnp.sin(fr * (_dot3(h, w2_ref[j]) + b2_ref[j]))
        h = _dot3(h, w3_ref[...])
        dec = dec_ref[...]
        row = lax.broadcasted_iota(jnp.int32, (L, HY_CH), 0)
        hf = h[:, 0:HY_CH] * dec
        hb = jnp.where(row == 0, 0.0, h[:, HY_CH:2 * HY_CH] * dec)
        a = hf + hb
        d = hf - hb
        sgn = jnp.where((row & 1) == 1, -1.0, 1.0)
        hn_ref[...] = jnp.sum(a * sgn, axis=0, keepdims=True) * (1.0 / n)
        ah, al = _split(a)
        dh, dl = _split(d)
        ah_scr[...] = ah
        al_scr[...] = al
        dh_scr[...] = dh
        dl_scr[...] = dl

    krow = lax.broadcasted_iota(jnp.int32, (FC, 1), 0) + i * FC
    wk = jnp.where(krow == 0, 1.0 / n, 2.0 / n)
    ch, cl, sh, sl = ch_ref[...], cl_ref[...], sh_ref[...], sl_ref[...]
    ah, al, dh, dl = ah_scr[...], al_scr[...], dh_scr[...], dl_scr[...]
    hr_ref[...] = (_dot(ch, ah) + _dot(ch, al) + _dot(cl, ah)) * wk
    hi_ref[...] = (_dot(sh, dh) + _dot(sh, dl) + _dot(sl, dh)) * wk


def _hyena_spectrum(L, zf, w1, b1, w2, b2, w3, fr, dec, mats):
    ch, cl, sh, sl = mats
    FC = min(512, L)
    mat_spec = pl.BlockSpec((FC, L), lambda i: (i, 0))
    out_spec = pl.BlockSpec((FC, HY_CH), lambda i: (i, 0))
    return pl.pallas_call(
        functools.partial(_filt_kernel, L=L, FC=FC),
        grid=(L // FC,),
        in_specs=[_const_spec(a.shape) for a in (zf, w1, b1, w2, b2, w3, fr, dec)] + [mat_spec] * 4,
        out_specs=[out_spec, out_spec, pl.BlockSpec((1, HY_CH), lambda i: (0, 0))],
        out_shape=[
            jax.ShapeDtypeStruct((L, HY_CH), F32),
            jax.ShapeDtypeStruct((L, HY_CH), F32),
            jax.ShapeDtypeStruct((1, HY_CH), F32),
        ],
        scratch_shapes=[pltpu.VMEM((L, HY_CH), BF16)] * 4,
        compiler_params=_cparams(("arbitrary",)),
        name="hyena_filter",
    )(zf, w1, b1, w2, b2, w3, fr, dec, ch, cl, sh, sl)


def _hy_kernel(v_ref, x1_ref, x0_ref, sw_ref, sb_ref, bias_ref, g_ref, c_ref, s_ref,
               hr_ref, hi_ref, hn_ref, o_ref, z_scr, x0_scr, zb_scr, yr_scr, yi_scr, *, L, R):
    nchunks = L // R
    zn = jnp.zeros((1, HY_CH), F32)
    for c in range(nchunks):
        r0 = c * R
        lo = max(r0 - 8, 0)
        hi = min(r0 + R + 8, L)
        rows = hi - lo
        off = r0 - lo
        grow = lax.broadcasted_iota(jnp.int32, (R, HY_CH), 0) + r0

        def conv(ref, c0):
            ext = ref[0, lo:hi, :]
            up = jnp.where(grow == 0, 0.0, pltpu.roll(ext, 1, 0)[off:off + R])
            un = jnp.where(grow == L - 1, 0.0, pltpu.roll(ext, rows - 1, 0)[off:off + R])
            u = ref[0, r0:r0 + R, :]
            return (up * sw_ref[0:1, c0:c0 + HY_CH] + u * sw_ref[1:2, c0:c0 + HY_CH]
                    + un * sw_ref[2:3, c0:c0 + HY_CH] + sb_ref[:, c0:c0 + HY_CH])

        z = conv(v_ref, 0) * conv(x1_ref, HY_CH)
        z_scr[r0:r0 + R, :] = z
        zb_scr[r0:r0 + R, :] = z.astype(BF16)
        x0_scr[r0:r0 + R, :] = conv(x0_ref, 2 * HY_CH)
        sgn = jnp.where((grow & 1) == 1, -1.0, 1.0)
        zn = zn + jnp.sum(z * sgn, axis=0, keepdims=True)

    zb = zb_scr[...]
    for c in range(nchunks):
        r0 = c * R
        zr = _dot(c_ref[r0:r0 + R, :], zb)
        zi = _dot(s_ref[r0:r0 + R, :], zb)
        hr = hr_ref[r0:r0 + R, :]
        hi_ = hi_ref[r0:r0 + R, :]
        yr_scr[r0:r0 + R, :] = (zr * hr - zi * hi_).astype(BF16)
        yi_scr[r0:r0 + R, :] = (zr * hi_ + zi * hr).astype(BF16)

    yr = yr_scr[...]
    yi = yi_scr[...]
    nyq = zn * hn_ref[...]
    for c in range(nchunks):
        r0 = c * R
        y = _dot(c_ref[r0:r0 + R, :], yr) + _dot(s_ref[r0:r0 + R, :], yi)
        trow = lax.broadcasted_iota(jnp.int32, (R, HY_CH), 0) + r0
        y = y + jnp.where((trow & 1) == 1, -nyq, nyq)
        y = y + z_scr[r0:r0 + R, :] * bias_ref[...]
        out = y * x0_scr[r0:r0 + R, :]
        o_ref[0, r0:r0 + R, :] = (_rms(out) * g_ref[...]).astype(BF16)


def _hyena(p32, short_w, short_b, hy_bias, g_hy, cmat, smat, hr, hi, hn):
    B, L, _ = p32.shape
    R = min(256, L)
    col = lambda j: pl.BlockSpec((1, L, HY_CH), lambda b: (b, 0, j))
    return pl.pallas_call(
        functools.partial(_hy_kernel, L=L, R=R),
        grid=(B,),
        in_specs=[col(1), col(2), col(3)] + [
            _const_spec(a.shape) for a in (short_w, short_b, hy_bias, g_hy, cmat, smat, hr, hi, hn)],
        out_specs=pl.BlockSpec((1, L, HY_CH), lambda b: (b, 0, 0)),
        out_shape=jax.ShapeDtypeStruct((B, L, HY_CH), BF16),
        scratch_shapes=[
            pltpu.VMEM((L, HY_CH), F32), pltpu.VMEM((L, HY_CH), F32),
            pltpu.VMEM((L, HY_CH), BF16), pltpu.VMEM((L, HY_CH), BF16), pltpu.VMEM((L, HY_CH), BF16),
        ],
        compiler_params=_cparams(("parallel",)),
        name="hyena",
    )(p32, p32, p32, short_w, short_b, hy_bias, g_hy, cmat, smat, hr, hi, hn)


def _expand_kv(kv_ref, k2_scr, vv_scr, rows):
    R = min(256, rows)
    lo = lax.broadcasted_iota(jnp.int32, (R, LANES), 1) < ATT_HD
    ones = jnp.ones((R, LANES), BF16)
    for c in range(rows // R):
        sl = slice(c * R, (c + 1) * R)
        k = kv_ref[0, sl, 0:128].astype(F32)
        v = kv_ref[0, sl, 128:256].astype(F32)
        kr = pltpu.roll(k, ATT_HD, 1)
        vr = pltpu.roll(v, ATT_HD, 1)
        k2_scr[0, sl, :] = jnp.where(lo, k, kr).astype(BF16)
        k2_scr[1, sl, :] = jnp.where(lo, kr, k).astype(BF16)
        vv_scr[0, sl, 0:128] = jnp.where(lo, v, vr).astype(BF16)
        vv_scr[1, sl, 0:128] = jnp.where(lo, vr, v).astype(BF16)
        vv_scr[0, sl, 128:256] = ones
        vv_scr[1, sl, 128:256] = ones


def _att_kernel(sink_ref, q_ref, *rest, L, Lc, band):
    if band:
        (kv_ref, ckv_ref, g_ref, o_ref, k2_scr, vv_scr, ck2_scr, cvv_scr, att_scr) = rest
        _expand_kv(kv_ref, k2_scr, vv_scr, L)
    else:
        (ckv_ref, g_ref, o_ref, ck2_scr, cvv_scr, att_scr) = rest
    _expand_kv(ckv_ref, ck2_scr, cvv_scr, Lc)

    T = ATT_BLOCK
    nb = L // T
    lane = lax.broadcasted_iota(jnp.int32, (1, LANES), 1)
    lo = lane < ATT_HD
    lo16 = jnp.where(lo, 1.0, 0.0).astype(BF16)
    hi16 = jnp.where(lo, 0.0, 1.0).astype(BF16)
    hrow = lax.broadcasted_iota(jnp.int32, (4 * T, 1), 0) >> 7
    ti = lax.broadcasted_iota(jnp.int32, (T, T), 0)
    tj = lax.broadcasted_iota(jnp.int32, (T, T), 1)
    tri_prev = jnp.where(tj >= ti, 0.0, NEG_INF)
    tri_next = jnp.where(tj <= ti, 0.0, NEG_INF)

    def add_bias(s, bias):
        return (s.reshape(4, T, T) + bias[None]).reshape(4 * T, T)

    U = min(ATT_UNROLL, nb)

    def step(it, carry):
        blocks = []
        for u in range(U):
            n = it * U + u
            blk = dict(r0=pl.multiple_of(n * T, T))
            if band:
                blk.update(rp=pl.multiple_of(jnp.maximum(n - 1, 0) * T, T),
                           rn=pl.multiple_of(jnp.minimum(n + 1, nb - 1) * T, T),
                           bias_p=tri_prev + jnp.where(n == 0, NEG_INF, 0.0),
                           bias_n=tri_next + jnp.where(n == nb - 1, NEG_INF, 0.0))
            blocks.append(blk)
        items = [(u, g) for u in range(U) for g in range(ATT_KV_HEADS)]

        def keys_of(scr, cscr, blk, g):
            parts = [cscr[g]]
            if band:
                parts += [scr[g, pl.ds(blk[k], T), :] for k in ("rp", "r0", "rn")]
            return jnp.concatenate(parts, axis=0)

        scores, sinks = [], []
        for u, g in items:
            qb = q_ref[0, pl.ds(blocks[u]["r0"], T), 256 * g:256 * (g + 1)]
            qs = []
            for p in range(2):
                qp = qb[:, 128 * p:128 * (p + 1)]
                qs += [qp * lo16, qp * hi16]
            qst = jnp.concatenate(qs, axis=0)
            scores.append(_dot_nt(qst, keys_of(k2_scr if band else None, ck2_scr, blocks[u], g)))
            sinks.append(LOG2E * jnp.where(
                hrow == 0, sink_ref[4 * g],
                jnp.where(hrow == 1, sink_ref[4 * g + 1],
                          jnp.where(hrow == 2, sink_ref[4 * g + 2], sink_ref[4 * g + 3]))))
        probs, esks = [], []
        for (u, g), s, sk in zip(items, scores, sinks):
            cols = [s[:, j * T:(j + 1) * T] for j in range(s.shape[1] // T)]
            if band:
                cols[-3] = add_bias(cols[-3], blocks[u]["bias_p"])
                cols[-1] = add_bias(cols[-1], blocks[u]["bias_n"])
            mx = cols[0]
            for c_ in cols[1:]:
                mx = jnp.maximum(mx, c_)
            m = jnp.maximum(jnp.max(mx, axis=-1, keepdims=True), sk)
            probs.append(jnp.concatenate([jnp.exp2(c_ - m).astype(BF16) for c_ in cols], axis=1))
            esks.append(jnp.exp2(sk - m))
        for (u, g), pr, esk in zip(items, probs, esks):
            pv = _dot(pr, keys_of(vv_scr if band else None, cvv_scr, blocks[u], g))
            res = pv[:, 0:LANES] / (pv[:, LANES:2 * LANES] + esk)
            for p in range(2):
                c0 = 128 * (2 * g + p)
                att_scr[u, :, c0:c0 + 128] = jnp.where(lo, res[2 * p * T:(2 * p + 1) * T],
                                                       res[(2 * p + 1) * T:(2 * p + 2) * T])
        for u in range(U):
            o_ref[0, pl.ds(blocks[u]["r0"], T), :] = (_rms(att_scr[u]) * g_ref[...]).astype(BF16)
        return carry

    lax.fori_loop(0, nb // U, step, 0)


def _attention(sink, p16, pc16, g_att, band):
    B, Lc, _ = pc16.shape
    L = p16.shape[1] if band else Lc
    qsrc = p16 if band else pc16
    kv_spec = lambda n: pl.BlockSpec((1, n, 256), lambda b: (b, 0, 4))
    in_specs = [pl.BlockSpec(memory_space=pltpu.SMEM),
                pl.BlockSpec((1, L, ATT_Q), lambda b: (b, 0, 1))]
    args = [sink, qsrc]
    scratch = []
    if band:
        in_specs.append(kv_spec(L))
        args.append(p16)
        scratch += [pltpu.VMEM((2, L, LANES), BF16), pltpu.VMEM((2, L, 2 * LANES), BF16)]
    in_specs += [kv_spec(Lc), _const_spec((1, ATT_Q))]
    args += [pc16, g_att]
    scratch += [pltpu.VMEM((2, Lc, LANES), BF16), pltpu.VMEM((2, Lc, 2 * LANES), BF16),
                pltpu.VMEM((ATT_UNROLL, ATT_BLOCK, ATT_Q), F32)]
    return pl.pallas_call(
        functools.partial(_att_kernel, L=L, Lc=Lc, band=band),
        grid=(B,),
        in_specs=in_specs,
        out_specs=pl.BlockSpec((1, L, ATT_Q), lambda b: (b, 0, 0)),
        out_shape=jax.ShapeDtypeStruct((B, L, ATT_Q), BF16),
        scratch_shapes=scratch,
        compiler_params=_cparams(("parallel",)),
        name="attn_window" if band else "attn_ctx",
    )(*args)


def _out_mlp_kernel(x_ref, nr_ref, nh_ref, na_ref, gta_ref, shm_ref, scm_ref, gtm_ref,
                    gpm_ref, gpre_ref, gpost_ref, wo_ref, w1_ref, w2_ref, o_ref, *, ff_chunk, sub):
    tm = x_ref.shape[1]
    dff = w1_ref.shape[1]

    def mixer_out(i):
        r = slice(i * sub, (i + 1) * sub)
        mix = (_dot(nr_ref[0, r, :], wo_ref[0:256, :]) + _dot(nh_ref[0, r, :], wo_ref[256:512, :])
               + _dot(na_ref[0, r, :], wo_ref[512:1024, :]))
        x1 = x_ref[0, r, :] + gta_ref[0] * (_rms(mix) * gpm_ref[...])
        h = ((_rms(x1) * gpre_ref[...]) * (1.0 + scm_ref[0]) + shm_ref[0]).astype(BF16)
        return x1, h

    def mlp(i, x1, h):
        acc = None
        for j in range(dff // ff_chunk):
            sl = slice(j * ff_chunk, (j + 1) * ff_chunk)
            hj = jnp.square(jnp.maximum(_dot(h, w1_ref[:, sl]), 0.0)).astype(BF16)
            part = _dot(hj, w2_ref[sl, :])
            acc = part if acc is None else acc + part
        o_ref[0, i * sub:(i + 1) * sub, :] = x1 + gtm_ref[0] * (_rms(acc) * gpost_ref[...])

    nxt = mixer_out(0)
    for i in range(tm // sub):
        cur = nxt
        if i + 1 < tm // sub:
            nxt = mixer_out(i + 1)
        mlp(i, *cur)


def _out_mlp(x3, nr, nh, na, mods, mod_row, g_post_mix, g_pre_mlp, g_post_mlp, wo16, w116, w216, tm):
    G, R, D = x3.shape
    row = lambda w: pl.BlockSpec((1, tm, w), lambda g, i: (g, i, 0))
    mod = lambda j: pl.BlockSpec((1, 1, D), lambda g, i: (mod_row(g), 0, j))
    return pl.pallas_call(
        functools.partial(_out_mlp_kernel, ff_chunk=1024, sub=min(512, tm)),
        grid=(G, R // tm),
        in_specs=[row(D), row(RET_V), row(HY_CH), row(ATT_Q), mod(2), mod(3), mod(4), mod(5),
                  _const_spec((1, D)), _const_spec((1, D)), _const_spec((1, D)),
                  _const_spec(wo16.shape), _const_spec(w116.shape), _const_spec(w216.shape)],
        out_specs=row(D),
        out_shape=jax.ShapeDtypeStruct((G, R, D), F32),
        compiler_params=_cparams(("parallel", "parallel")),
        name="out_mlp",
    )(x3, nr, nh, na, mods, mods, mods, mods, g_post_mix, g_pre_mlp, g_post_mlp, wo16, w116, w216)


def _rot_tables(cos, sin):
    lane = jnp.arange(LANES) % 32
    first = (lane < 16)[None, :]
    return cos, jnp.where(first, -sin, 0.0), jnp.where(first, 0.0, sin)


def _rotary_tables(L):
    pos = jnp.arange(L)
    row = (pos // GRID_W).astype(F32)
    col = (pos % GRID_W).astype(F32)
    half = ATT_HD // 2
    inv_ax = ROPE_BASE ** (-jnp.arange(0, half, 2, dtype=F32) / half)

    def tab(p, inv):
        ang = p[:, None] * inv[None, :]
        ang = jnp.concatenate([ang, ang], axis=-1)
        return jnp.cos(ang), jnp.sin(ang)

    cr, sr = tab(row, inv_ax)
    cc, sc = tab(col, inv_ax)
    inv_ret = 1.0 / (RET_ROT_BASE ** jnp.linspace(0.0, 1.0, RET_DK // 2, dtype=F32))
    ct, st = tab(pos.astype(F32), inv_ret)
    ret = _rot_tables(jnp.tile(ct, (1, 4)), jnp.tile(st, (1, 4)))
    ax = _rot_tables(jnp.tile(jnp.concatenate([cr, cc], -1), (1, 2)),
                     jnp.tile(jnp.concatenate([sr, sc], -1), (1, 2)))
    return ret + ax


def _dft_mats(L):
    n = 2 * L
    k = jnp.arange(L, dtype=jnp.int32)
    ang = ((k[:, None] * k[None, :]) % n).astype(F32) * (2.0 * math.pi / n)
    c = jnp.cos(ang)
    s = -jnp.sin(ang)
    ch, sh = c.astype(BF16), s.astype(BF16)
    cl = (c - ch.astype(F32)).astype(BF16)
    sl = (s - sh.astype(F32)).astype(BF16)
    return ch, cl, sh, sl


def _filter_features(L):
    t = jnp.linspace(0.0, 1.0, L, dtype=F32)[:, None]
    w = 2.0 * math.pi * jnp.arange(L, dtype=F32) / L
    bands = jnp.linspace(1e-4, HY_BANDS - 1, HY_BANDS, dtype=F32)
    ang = w[:, None] * bands[None, :]
    z = jnp.concatenate([t, jnp.cos(ang), -jnp.sin(ang)], axis=-1)
    zf = jnp.pad(z, ((0, 0), (0, LANES - HY_EMB)))
    max_decay = math.log(HY_TARGET) / HY_FAST_DECAY
    min_decay = math.log(HY_TARGET) / HY_SLOW_DECAY
    deltas = jnp.linspace(min_decay, max_decay, HY_CH, dtype=F32)
    dec = jnp.exp(-t * jnp.abs(deltas)[None, :])
    return zf, dec


def _row_tile(rows):
    return 1024 if rows % 1024 == 0 else 512


def _pad_to(a, shape):
    return jnp.pad(a, [(0, s - d) for d, s in zip(a.shape, shape)])


def kernel(x, c, ctx, c_ctx, w_ada, b_ada, g_pre_mix, g_post_mix, g_pre_mlp, g_post_mlp, w_in,
           ret_decay_fwd, ret_decay_bwd, hy_short_w, hy_short_b, hy_f_w1, hy_f_b1, hy_f_w2, hy_f_b2,
           hy_f_w3, hy_f_freq, hy_bias, attn_sink, g_ret, g_hy, g_att, w_out, w_ff1, w_ff2):
    B, L, D = x.shape
    Lc = ctx.shape[1]
    depth = w_ada.shape[0]
    assert D == 1024 and w_in.shape[2] == 2304 and L % 256 == 0 and Lc % 256 == 0 and L >= 3 * ATT_BLOCK

    rows = -(-(B + 1) // 8) * 8
    cc = _pad_to(jnp.concatenate([c, c_ctx[None, :]], axis=0), (rows, D))
    mods_all = _ada(cc, w_ada, b_ada).reshape(depth, rows, 1, 6 * D)

    w_in16 = w_in.astype(BF16)
    w_out16 = w_out.astype(BF16)
    w_ff116 = w_ff1.astype(BF16)
    w_ff216 = w_ff2.astype(BF16)

    tabs = _rotary_tables(L)
    mats = {n: _dft_mats(n) for n in {L, Lc}}
    feats = {n: _filter_features(n) for n in {L, Lc}}

    ctx_rows = B * Lc
    tm = _row_tile(L)
    tc = _row_tile(ctx_rows)
    xc = ctx.reshape(ctx_rows // tc, tc, D)
    lat_row = lambda g: g
    ctx_row = lambda g: B

    for l in range(depth):
        last = l == depth - 1
        mods = mods_all[l]
        dec2 = jnp.stack([ret_decay_fwd[l], ret_decay_bwd[l]])
        dq = jnp.repeat(dec2, RET_DK, axis=1)
        ds = jnp.repeat(dec2, RET_CHUNK, axis=1)
        dv = jnp.repeat(dec2, RET_DV, axis=1)
        gr, gh, ga = g_ret[l][None], g_hy[l][None], g_att[l][None]
        gpre, gpm = g_pre_mix[l][None], g_post_mix[l][None]
        gprm, gpom = g_pre_mlp[l][None], g_post_mlp[l][None]
        fw1 = _pad_to(hy_f_w1[l], (LANES, LANES))
        fb1 = _pad_to(hy_f_b1[l][None], (1, LANES))
        fw2 = _pad_to(hy_f_w2[l], (2, LANES, LANES))
        fb2 = _pad_to(hy_f_b2[l][:, None, :], (2, 1, LANES))
        fw3 = _pad_to(hy_f_w3[l], (LANES, 2 * HY_CH))
        ffr = _pad_to(hy_f_freq[l][None], (1, LANES))
        sw, sb, hbias = hy_short_w[l], hy_short_b[l][None], hy_bias[l][None]

        def spectrum(n):
            zf, dec = feats[n]
            return _hyena_spectrum(n, zf, fw1, fb1, fw2, fb2, fw3, ffr, dec, mats[n])

        pc16, pc32 = _in_proj(xc, mods, ctx_row, gpre, w_in16[l], None, tc)
        pc16 = pc16.reshape(B, Lc, P16_W)
        pc32 = pc32.reshape(B, Lc, P32_W)
        cret, cstate = _retention(pc16, pc32, dq, ds, dv, gr, None)

        p16, p32 = _in_proj(x, mods, lat_row, gpre, w_in16[l], tabs, tm)
        ret, _ = _retention(p16, p32, dq, ds, dv, gr, cstate)
        hr, hi, hn = spectrum(L)
        hyo = _hyena(p32, sw, sb, hbias, gh, mats[L][0], mats[L][2], hr, hi, hn)
        att = _attention(attn_sink[l], p16, pc16, ga, True)
        x = _out_mlp(x, ret, hyo, att, mods, lat_row, gpm, gprm, gpom,
                     w_out16[l], w_ff116[l], w_ff216[l], tm)

        if not last:
            chr_, chi, chn = spectrum(Lc)
            chyo = _hyena(pc32, sw, sb, hbias, gh, mats[Lc][0], mats[Lc][2], chr_, chi, chn)
            catt = _attention(attn_sink[l], None, pc16, ga, False)
            r3 = lambda a: a.reshape(ctx_rows // tc, tc, a.shape[-1])
            xc = _out_mlp(xc, r3(cret), r3(chyo), r3(catt), mods, ctx_row, gpm, gprm, gpom,
                          w_out16[l], w_ff116[l], w_ff216[l], tc)
    return x
```

```python
import functools
import math

import jax
import jax.numpy as jnp
from jax import lax
from jax.experimental import pallas as pl
from jax.experimental.pallas import tpu as pltpu

F32 = jnp.float32
BF16 = jnp.bfloat16
EPS = 1e-6
NEG_INF = -1e30
LOG2E = math.log2(math.e)

RET_HEADS = 4
RET_DK = 32
RET_DV = 64
RET_QK = RET_HEADS * RET_DK
RET_V = RET_HEADS * RET_DV
RET_CHUNK = 128
RET_UNROLL = 4
RET_ROT_BASE = 10000.0
HY_CH = 256
HY_EMB = 33
HY_BANDS = 16
HY_ORDER = 64
HY_FAST_DECAY = 0.3
HY_SLOW_DECAY = 1.5
HY_TARGET = 1e-2
ATT_HEADS = 8
ATT_KV_HEADS = 2
ATT_HD = 64
ATT_Q = ATT_HEADS * ATT_HD
ATT_KV = ATT_KV_HEADS * ATT_HD
WINDOW = 128
ATT_BLOCK = 128
ATT_UNROLL = 4
ROPE_BASE = 10000.0
GRID_W = 64

P16_W = 2 * RET_QK + RET_V + ATT_Q + 2 * ATT_KV
P32_W = RET_V + 3 * HY_CH

LANES = 128
VMEM_LIMIT = 56 * 1024 * 1024


def _cparams(sem):
    return pltpu.CompilerParams(dimension_semantics=sem, vmem_limit_bytes=VMEM_LIMIT)


def _const_spec(shape):
    nd = len(shape)
    return pl.BlockSpec(shape, lambda *_: (0,) * nd, pipeline_mode=pl.Buffered(1))


def _layer_spec(stacked, l):
    nd = stacked.ndim - 1
    return pl.BlockSpec((None,) + stacked.shape[1:], lambda *_: (l,) + (0,) * nd,
                        pipeline_mode=pl.Buffered(1))


def _rms(x):
    return x * lax.rsqrt(jnp.mean(x * x, axis=-1, keepdims=True) + EPS)


def _dot(a, b):
    return jnp.dot(a, b, preferred_element_type=F32)


def _dot_nt(a, b):
    return lax.dot_general(a, b, (((1,), (1,)), ((), ())), preferred_element_type=F32)


def _dot_tn(a, b):
    return lax.dot_general(a, b, (((0,), (0,)), ((), ())), preferred_element_type=F32)


def _split(a):
    hi = a.astype(BF16)
    lo = (a - hi.astype(F32)).astype(BF16)
    return hi, lo


def _dot3(a, b):
    ah, al = _split(a)
    bh, bl = _split(b)
    return _dot(ah, bh) + _dot(al, bh) + _dot(ah, bl)


def _ada_kernel(c_ref, w_ref, b_ref, o_ref):
    cv = c_ref[...]
    s = cv * jax.nn.sigmoid(cv)
    o_ref[0] = _dot(s.astype(BF16), w_ref[0].astype(BF16)) + b_ref[0]


def _ada(cc, w_ada, b_ada):
    depth, d, d6 = w_ada.shape
    rows = cc.shape[0]
    tn = 1536
    return pl.pallas_call(
        _ada_kernel,
        grid=(depth, d6 // tn),
        in_specs=[
            pl.BlockSpec((rows, d), lambda l, j: (0, 0)),
            pl.BlockSpec((1, d, tn), lambda l, j: (l, 0, j)),
            pl.BlockSpec((1, 1, tn), lambda l, j: (l, 0, j)),
        ],
        out_specs=pl.BlockSpec((1, rows, tn), lambda l, j: (l, 0, j)),
        out_shape=jax.ShapeDtypeStruct((depth, rows, d6), F32),
        compiler_params=_cparams(("arbitrary", "arbitrary")),
        name="ada",
    )(cc, w_ada, b_ada.reshape(depth, 1, d6))


def _rot(x, cos, sa, sb):
    return x * cos + pltpu.roll(x, LANES - 16, 1) * sa + pltpu.roll(x, 16, 1) * sb


def _in_kernel(x_ref, sh_ref, sc_ref, g_ref, w_ref, *rest, rotary, sub):
    if rotary:
        rc, rsa, rsb, ac, asa, asb, o16_ref, o32_ref = rest
    else:
        o16_ref, o32_ref = rest
    tm = x_ref.shape[1]
    k_scale = RET_DK ** -0.5
    q_scale = ATT_HD ** -0.5 * LOG2E

    def normed(i):
        x = x_ref[0, i * sub:(i + 1) * sub, :]
        return ((_rms(x) * g_ref[...]) * (1.0 + sc_ref[0]) + sh_ref[0]).astype(BF16)

    def project(i, hb):
        r = slice(i * sub, (i + 1) * sub)

        def rot(v, tabs):
            return _rot(v, *[t[r, :] for t in tabs]) if rotary else v

        ret_tabs = (rc, rsa, rsb) if rotary else None
        ax_tabs = (ac, asa, asb) if rotary else None
        pr = _dot(hb, w_ref[:, 0:512])
        o16_ref[0, r, 0:128] = rot(pr[:, 0:128], ret_tabs).astype(BF16)
        o16_ref[0, r, 128:256] = (rot(pr[:, 128:256], ret_tabs) * k_scale).astype(BF16)
        o16_ref[0, r, 256:512] = pr[:, 256:512].astype(BF16)
        o32_ref[0, r, :] = _dot(hb, w_ref[:, 512:1536])
        pa = _dot(hb, w_ref[:, 1536:2304])
        for j in range(4):
            aq = rot(pa[:, 128 * j:128 * (j + 1)], ax_tabs)
            o16_ref[0, r, 512 + 128 * j:640 + 128 * j] = (aq * q_scale).astype(BF16)
        o16_ref[0, r, 1024:1152] = rot(pa[:, 512:640], ax_tabs).astype(BF16)
        o16_ref[0, r, 1152:1280] = pa[:, 640:768].astype(BF16)

    nxt = normed(0)
    for i in range(tm // sub):
        cur = nxt
        if i + 1 < tm // sub:
            nxt = normed(i + 1)
        project(i, cur)


def _in_proj(x3, mods, mod_row, g_pre, w_in16, l, tabs, tm):
    G, R, D = x3.shape
    rotary = tabs is not None
    nt = R // tm
    in_specs = [
        pl.BlockSpec((1, tm, D), lambda g, i: (g, i, 0)),
        pl.BlockSpec((1, 1, D), lambda g, i: (mod_row(g), 0, 0)),
        pl.BlockSpec((1, 1, D), lambda g, i: (mod_row(g), 0, 1)),
        _const_spec((1, D)),
        _layer_spec(w_in16, l),
    ]
    args = [x3, mods, mods, g_pre, w_in16]
    if rotary:
        in_specs += [pl.BlockSpec((tm, LANES), lambda g, i: (i, 0))] * 6
        args += list(tabs)
    return pl.pallas_call(
        functools.partial(_in_kernel, rotary=rotary, sub=min(256, tm)),
        grid=(G, nt),
        in_specs=in_specs,
        out_specs=[
            pl.BlockSpec((1, tm, P16_W), lambda g, i: (g, i, 0)),
            pl.BlockSpec((1, tm, P32_W), lambda g, i: (g, i, 0)),
        ],
        out_shape=[
            jax.ShapeDtypeStruct((G, R, P16_W), BF16),
            jax.ShapeDtypeStruct((G, R, P32_W), F32),
        ],
        compiler_params=_cparams(("parallel", "parallel")),
        name="in_proj_rot" if rotary else "in_proj",
    )(*args)


def _ret_kernel(qkv_ref, rg_ref, dq_ref, ds_ref, dv_ref, g_ref, *rest, L, has_init):
    if has_init:
        s0_ref, o_ref, sfin_ref, st_scr = rest
    else:
        o_ref, sfin_ref, st_scr = rest
    C = RET_CHUNK
    N = L // C

    def log_gamma(ref, r):
        return jnp.log1p(-jnp.exp(ref[r:r + 1, :]))

    lfq, lbq = log_gamma(dq_ref, 0), log_gamma(dq_ref, 1)
    lfs, lbs = log_gamma(ds_ref, 0), log_gamma(ds_ref, 1)
    lfv, lbv = log_gamma(dv_ref, 0), log_gamma(dv_ref, 1)
    ri = lax.broadcasted_iota(jnp.int32, (C, LANES), 0).astype(F32)
    qdec_f = jnp.exp(lfq * (ri + 1.0))
    kdec_f = jnp.exp(lfq * (C - 1.0 - ri))
    qdec_b = jnp.exp(lbq * (C - ri))
    kdec_b = jnp.exp(lbq * ri)
    di = lax.broadcasted_iota(jnp.int32, (C, 4 * C), 0)
    dj = lax.broadcasted_iota(jnp.int32, (C, 4 * C), 1) & (C - 1)
    diff = (di - dj).astype(F32)
    dmask = (jnp.where(diff >= 0, jnp.exp(lfs * jnp.maximum(diff, 0.0)), 0.0)
             + jnp.where(diff <= 0, jnp.exp(lbs * jnp.maximum(-diff, 0.0)), 0.0))
    cdec_f = jnp.exp(lfv * float(C))
    cdec_b = jnp.exp(lbv * float(C))

    lane_q = lax.broadcasted_iota(jnp.int32, (1, RET_QK), 1) >> 5
    lane_v = lax.broadcasted_iota(jnp.int32, (1, RET_V), 1) >> 6
    hm = [jnp.where(lane_q == h, 1.0, 0.0).astype(BF16) for h in range(RET_HEADS)]
    cm = [jnp.where(lane_v == h, 1.0, 0.0).astype(BF16) for h in range(RET_HEADS)]
    bd = ((lax.broadcasted_iota(jnp.int32, (RET_QK, RET_V), 0) >> 5)
          == (lax.broadcasted_iota(jnp.int32, (RET_QK, RET_V), 1) >> 6))
    ones64 = jnp.where((lax.broadcasted_iota(jnp.int32, (RET_V, RET_V), 0) >> 6)
                       == (lax.broadcasted_iota(jnp.int32, (RET_V, RET_V), 1) >> 6), 1.0, 0.0).astype(BF16)

    if has_init:
        sf0 = s0_ref[0, 0]
        sb0 = s0_ref[0, 1]
    else:
        sf0 = jnp.zeros((RET_QK, RET_V), F32)
        sb0 = jnp.zeros((RET_QK, RET_V), F32)

    def kv_update(k16, v16, kdec, cdec, s):
        kd = (k16.astype(F32) * kdec).astype(BF16)
        return cdec * s + jnp.where(bd, _dot_tn(kd, v16), 0.0)

    def kv_at(n):
        r0 = pl.multiple_of(n * C, C)
        return qkv_ref[0, pl.ds(r0, C), 128:256], qkv_ref[0, pl.ds(r0, C), 256:512]

    U = min(RET_UNROLL, N)

    def scan_body(it, carry):
        sf, sb = carry
        for u in range(U):
            nf = it * U + u
            nb = N - 1 - nf
            st_scr[nf, 0:RET_QK, :] = sf.astype(BF16)
            sf = kv_update(*kv_at(nf), kdec_f, cdec_f, sf)
            st_scr[nb, RET_QK:2 * RET_QK, :] = sb.astype(BF16)
            sb = kv_update(*kv_at(nb), kdec_b, cdec_b, sb)
        return sf, sb

    sf_fin, sb_fin = lax.fori_loop(0, N // U, scan_body, (sf0, sb0))
    sfin_ref[0, 0] = sf_fin
    sfin_ref[0, 1] = sb_fin

    ones2 = jnp.concatenate([ones64, ones64], axis=0)

    def out_body(it, carry):
        ns = [it * U + u for u in range(U)]
        rows = [pl.ds(pl.multiple_of(n * C, C), C) for n in ns]
        q16s = [qkv_ref[0, r, 0:128] for r in rows]
        kvs = [kv_at(n) for n in ns]
        scores = [_dot_nt(q16, jnp.concatenate([k16 * hm[h] for h in range(RET_HEADS)], axis=0))
                  for q16, (k16, _) in zip(q16s, kvs)]
        outs = []
        for n, q16, (_, v16), s in zip(ns, q16s, kvs, scores):
            qf = q16.astype(F32)
            vblk = jnp.concatenate([v16 * cm[h] for h in range(RET_HEADS)], axis=0)
            lhs = jnp.concatenate([(s * dmask).astype(BF16), (qf * qdec_f).astype(BF16),
                                   (qf * qdec_b).astype(BF16)], axis=1)
            outs.append(_dot(lhs, jnp.concatenate([vblk, st_scr[n]], axis=0)))
        mss = []
        for o in outs:
            o2h, o2l = _split(o * o)
            mss.append(_dot(jnp.concatenate([o2h, o2l], axis=1), ones2) * (1.0 / RET_DV))
        for r, o, ms in zip(rows, outs, mss):
            rg = rg_ref[0, r, :]
            gated = (o * lax.rsqrt(ms + EPS)) * (rg * jax.nn.sigmoid(rg))
            o_ref[0, r, :] = (_rms(gated) * g_ref[...]).astype(BF16)
        return carry

    lax.fori_loop(0, N // U, out_body, 0)


def _retention(p16, p32, dq, ds, dv, g_ret, s0):
    B, L, _ = p16.shape
    has_init = s0 is not None
    in_specs = [
        pl.BlockSpec((1, L, 512), lambda b: (b, 0, 0)),
        pl.BlockSpec((1, L, RET_V), lambda b: (b, 0, 0)),
        _const_spec(dq.shape), _const_spec(ds.shape), _const_spec(dv.shape),
        _const_spec((1, RET_V)),
    ]
    args = [p16, p32, dq, ds, dv, g_ret]
    if has_init:
        in_specs.append(pl.BlockSpec((1, 2, RET_QK, RET_V), lambda b: (b, 0, 0, 0)))
        args.append(s0)
    return pl.pallas_call(
        functools.partial(_ret_kernel, L=L, has_init=has_init),
        grid=(B,),
        in_specs=in_specs,
        out_specs=[
            pl.BlockSpec((1, L, RET_V), lambda b: (b, 0, 0)),
            pl.BlockSpec((1, 2, RET_QK, RET_V), lambda b: (b, 0, 0, 0)),
        ],
        out_shape=[
            jax.ShapeDtypeStruct((B, L, RET_V), BF16),
            jax.ShapeDtypeStruct((B, 2, RET_QK, RET_V), F32),
        ],
        scratch_shapes=[pltpu.VMEM((L // RET_CHUNK, 2 * RET_QK, RET_V), BF16)],
        compiler_params=_cparams(("parallel",)),
        name="retention_init" if has_init else "retention",
    )(*args)


def _halves(scr, start, rows):
    return jnp.concatenate([scr[0, pl.ds(start, rows, stride=2), :],
                            scr[1, pl.ds(start, rows, stride=2), :]], axis=1)


def _alt_sign(rows, first=0):
    r = lax.broadcasted_iota(jnp.int32, (rows, HY_CH), 0) + first
    return jnp.where((r & 1) == 1, -1.0, 1.0)


def _filt_kernel(zf_ref, w1_ref, b1_ref, w2_ref, b2_ref, w3_ref, fr_ref, dec_ref,
                 ce_ref, se_ref, co_ref, so_ref, hra_ref, hia_ref, hrb_ref, hib_ref, hm_ref,
                 a_scr, d_scr, *, L):
    n = 2 * L
    M = L // 2
    fr = fr_ref[...]
    h = jnp.sin(fr * (_dot3(zf_ref[...], w1_ref[...]) + b1_ref[...]))
    for j in range(2):
        h = jnp.sin(fr * (_dot3(h, w2_ref[j]) + b2_ref[j]))
    h = _dot3(h, w3_ref[...])
    dec = dec_ref[...]
    row = lax.broadcasted_iota(jnp.int32, (L, HY_CH), 0)
    hf = h[:, 0:HY_CH] * dec
    hb = jnp.where(row == 0, 0.0, h[:, HY_CH:2 * HY_CH] * dec)
    for scr, val in ((a_scr, hf + hb), (d_scr, hf - hb)):
        scr[0] = val[:, 0:LANES]
        scr[1] = val[:, LANES:2 * LANES]
    ae, ao = _halves(a_scr, 0, M), _halves(a_scr, 1, M)
    de, do = _halves(d_scr, 0, M), _halves(d_scr, 1, M)
    sgn = _alt_sign(M)
    hm_ref[0:1, :] = jnp.sum(ae * sgn, axis=0, keepdims=True) * (2.0 / n)
    hm_ref[1:2, :] = jnp.sum(do * sgn, axis=0, keepdims=True) * (-2.0 / n)

    def dft(m_ref, x):
        xh, xl = _split(x)
        return _dot(m_ref[...], xh) + _dot(m_ref[...], xl)

    wk = jnp.where(lax.broadcasted_iota(jnp.int32, (M, 1), 0) == 0, 1.0 / n, 2.0 / n)
    ea, ta = dft(ce_ref, ae), dft(co_ref, ao)
    hra_ref[...] = (ea + ta) * wk
    hrb_ref[...] = (ea - ta) * wk
    ed, td = dft(se_ref, de), dft(so_ref, do)
    hia_ref[...] = (ed + td) * wk
    hib_ref[...] = (td - ed) * wk


def _hyena_spectrum(L, zf, w1, b1, w2, b2, w3, fr, dec, mats):
    M = L // 2
    args = (zf, w1, b1, w2, b2, w3, fr, dec) + tuple(mats[:4])
    half = jax.ShapeDtypeStruct((M, HY_CH), F32)
    return pl.pallas_call(
        functools.partial(_filt_kernel, L=L),
        grid=(1,),
        in_specs=[_const_spec(a.shape) for a in args],
        out_specs=[_const_spec((M, HY_CH))] * 4 + [_const_spec((2, HY_CH))],
        out_shape=[half] * 4 + [jax.ShapeDtypeStruct((2, HY_CH), F32)],
        scratch_shapes=[pltpu.VMEM((2, L, LANES), F32)] * 2,
        compiler_params=_cparams(("arbitrary",)),
        name="hyena_filter",
    )(*args)


def _hy_kernel(v_ref, x1_ref, x0_ref, sw_ref, sb_ref, bias_ref, g_ref,
               ce_ref, se_ref, co_ref, so_ref, cot_ref, sot_ref,
               hra_ref, hia_ref, hrb_ref, hib_ref, hm_ref, o_ref,
               z_scr, x0_scr, out_scr, ger_scr, gei_scr, gor_scr, goi_scr, *, L, R):
    nchunks = L // R
    M = L // 2
    RB = min(256, M)
    for c in range(nchunks):
        r0 = c * R
        lo = max(r0 - 8, 0)
        hi = min(r0 + R + 8, L)
        rows = hi - lo
        off = r0 - lo
        grow = lax.broadcasted_iota(jnp.int32, (R, HY_CH), 0) + r0

        def conv(ref, c0):
            ext = ref[0, lo:hi, :]
            up = pltpu.roll(ext, 1, 0)[off:off + R]
            un = pltpu.roll(ext, rows - 1, 0)[off:off + R]
            if c == 0:
                up = jnp.where(grow == 0, 0.0, up)
            if c == nchunks - 1:
                un = jnp.where(grow == L - 1, 0.0, un)
            u = ref[0, r0:r0 + R, :]
            return (up * sw_ref[0:1, c0:c0 + HY_CH] + u * sw_ref[1:2, c0:c0 + HY_CH]
                    + un * sw_ref[2:3, c0:c0 + HY_CH] + sb_ref[:, c0:c0 + HY_CH])

        z = conv(v_ref, 0) * conv(x1_ref, HY_CH)
        x0 = conv(x0_ref, 2 * HY_CH)
        for hlf in range(2):
            z_scr[hlf, r0:r0 + R, :] = z[:, hlf * LANES:(hlf + 1) * LANES]
            x0_scr[hlf, r0:r0 + R, :] = x0[:, hlf * LANES:(hlf + 1) * LANES]

    ze, zo = _halves(z_scr, 0, M), _halves(z_scr, 1, M)
    sgn = _alt_sign(M)
    mid_r = jnp.sum(ze * sgn, axis=0, keepdims=True)
    mid_i = -jnp.sum(zo * sgn, axis=0, keepdims=True)
    ym_r = mid_r * hm_ref[0:1, :] - mid_i * hm_ref[1:2, :]
    ym_i = mid_r * hm_ref[1:2, :] + mid_i * hm_ref[0:1, :]
    zeb, zob = ze.astype(BF16), zo.astype(BF16)
    for kb in range(M // RB):
        ks = slice(kb * RB, (kb + 1) * RB)
        er, ei = _dot(ce_ref[ks, :], zeb), _dot(se_ref[ks, :], zeb)
        tr, ti = _dot(co_ref[ks, :], zob), _dot(so_ref[ks, :], zob)
        ar, ai, br, bi = er + tr, ei + ti, er - tr, ti - ei
        hra, hia, hrb, hib = hra_ref[ks, :], hia_ref[ks, :], hrb_ref[ks, :], hib_ref[ks, :]
        yar, yai = ar * hra - ai * hia, ar * hia + ai * hra
        ybr, ybi = br * hrb - bi * hib, br * hib + bi * hrb
        ger_scr[ks, :] = (yar + ybr).astype(BF16)
        gei_scr[ks, :] = (yai - ybi).astype(BF16)
        gor_scr[ks, :] = (yar - ybr).astype(BF16)
        goi_scr[ks, :] = (yai + ybi).astype(BF16)

    ger, gei, gor, goi = ger_scr[...], gei_scr[...], gor_scr[...], goi_scr[...]
    for tb in range(M // RB):
        ts = slice(tb * RB, (tb + 1) * RB)
        sg = _alt_sign(RB, tb * RB)
        ys = (_dot(ce_ref[ts, :], ger) + _dot(se_ref[ts, :], gei) + sg * ym_r,
              _dot(cot_ref[ts, :], gor) + _dot(sot_ref[ts, :], goi) - sg * ym_i)
        for par, y in enumerate(ys):
            first = 2 * tb * RB + par
            out = (y + _halves(z_scr, first, RB) * bias_ref[...]) * _halves(x0_scr, first, RB)
            out = _rms(out) * g_ref[...]
            for hlf in range(2):
                out_scr[hlf, pl.ds(first, RB, stride=2), :] = out[:, hlf * LANES:(hlf + 1) * LANES]

    for c in range(nchunks):
        r = slice(c * R, (c + 1) * R)
        o_ref[0, r, :] = jnp.concatenate([out_scr[0, r, :], out_scr[1, r, :]], axis=1).astype(BF16)


def _hyena(p32, short_w, short_b, hy_bias, g_hy, mats, spec):
    B, L, _ = p32.shape
    R = min(256, L)
    M = L // 2
    col = lambda j: pl.BlockSpec((1, L, HY_CH), lambda b: (b, 0, j))
    consts = (short_w, short_b, hy_bias, g_hy) + tuple(mats) + tuple(spec)
    return pl.pallas_call(
        functools.partial(_hy_kernel, L=L, R=R),
        grid=(B,),
        in_specs=[col(1), col(2), col(3)] + [_const_spec(a.shape) for a in consts],
        out_specs=pl.BlockSpec((1, L, HY_CH), lambda b: (b, 0, 0)),
        out_shape=jax.ShapeDtypeStruct((B, L, HY_CH), BF16),
        scratch_shapes=[pltpu.VMEM((2, L, LANES), F32)] * 3 + [pltpu.VMEM((M, HY_CH), BF16)] * 4,
        compiler_params=_cparams(("parallel",)),
        name="hyena",
    )(p32, p32, p32, *consts)


def _expand_kv(kv_ref, k2_scr, vv_scr, rows):
    R = min(256, rows)
    lo = lax.broadcasted_iota(jnp.int32, (R, LANES), 1) < ATT_HD
    ones = jnp.ones((R, LANES), BF16)
    for c in range(rows // R):
        sl = slice(c * R, (c + 1) * R)
        k = kv_ref[0, sl, 0:128].astype(F32)
        v = kv_ref[0, sl, 128:256].astype(F32)
        kr = pltpu.roll(k, ATT_HD, 1)
        vr = pltpu.roll(v, ATT_HD, 1)
        k2_scr[0, sl, :] = jnp.where(lo, k, kr).astype(BF16)
        k2_scr[1, sl, :] = jnp.where(lo, kr, k).astype(BF16)
        vv_scr[0, sl, 0:128] = jnp.where(lo, v, vr).astype(BF16)
        vv_scr[1, sl, 0:128] = jnp.where(lo, vr, v).astype(BF16)
        vv_scr[0, sl, 128:256] = ones
        vv_scr[1, sl, 128:256] = ones


def _att_kernel(sink_ref, q_ref, *rest, L, Lc, band):
    if band:
        (kv_ref, ckv_ref, g_ref, o_ref, k2_scr, vv_scr, ck2_scr, cvv_scr, att_scr) = rest
        _expand_kv(kv_ref, k2_scr, vv_scr, L)
    else:
        (ckv_ref, g_ref, o_ref, ck2_scr, cvv_scr, att_scr) = rest
    _expand_kv(ckv_ref, ck2_scr, cvv_scr, Lc)

    T = ATT_BLOCK
    nb = L // T
    lane = lax.broadcasted_iota(jnp.int32, (1, LANES), 1)
    lo = lane < ATT_HD
    lo16 = jnp.where(lo, 1.0, 0.0).astype(BF16)
    hi16 = jnp.where(lo, 0.0, 1.0).astype(BF16)
    hrow = lax.broadcasted_iota(jnp.int32, (4 * T, 1), 0) >> 7
    ti = lax.broadcasted_iota(jnp.int32, (T, T), 0)
    tj = lax.broadcasted_iota(jnp.int32, (T, T), 1)
    tri_prev = jnp.where(tj >= ti, 0.0, NEG_INF)
    tri_next = jnp.where(tj <= ti, 0.0, NEG_INF)

    def add_bias(s, bias):
        return (s.reshape(4, T, T) + bias[None]).reshape(4 * T, T)

    U = min(ATT_UNROLL, nb)

    def step(it, carry):
        blocks = []
        for u in range(U):
            n = it * U + u
            blk = dict(r0=pl.multiple_of(n * T, T))
            if band:
                blk.update(rp=pl.multiple_of(jnp.maximum(n - 1, 0) * T, T),
                           rn=pl.multiple_of(jnp.minimum(n + 1, nb - 1) * T, T),
                           bias_p=tri_prev + jnp.where(n == 0, NEG_INF, 0.0),
                           bias_n=tri_next + jnp.where(n == nb - 1, NEG_INF, 0.0))
            blocks.append(blk)
        items = [(u, g) for u in range(U) for g in range(ATT_KV_HEADS)]

        def keys_of(scr, cscr, blk, g):
            parts = [cscr[g]]
            if band:
                parts += [scr[g, pl.ds(blk[k], T), :] for k in ("rp", "r0", "rn")]
            return jnp.concatenate(parts, axis=0)

        scores, sinks = [], []
        for u, g in items:
            qb = q_ref[0, pl.ds(blocks[u]["r0"], T), 256 * g:256 * (g + 1)]
            qs = []
            for p in range(2):
                qp = qb[:, 128 * p:128 * (p + 1)]
                qs += [qp * lo16, qp * hi16]
            qst = jnp.concatenate(qs, axis=0)
            scores.append(_dot_nt(qst, keys_of(k2_scr if band else None, ck2_scr, blocks[u], g)))
            sinks.append(LOG2E * jnp.where(
                hrow == 0, sink_ref[4 * g],
                jnp.where(hrow == 1, sink_ref[4 * g + 1],
                          jnp.where(hrow == 2, sink_ref[4 * g + 2], sink_ref[4 * g + 3]))))
        probs, esks = [], []
        for (u, g), s, sk in zip(items, scores, sinks):
            cols = [s[:, j * T:(j + 1) * T] for j in range(s.shape[1] // T)]
            if band:
                cols[-3] = add_bias(cols[-3], blocks[u]["bias_p"])
                cols[-1] = add_bias(cols[-1], blocks[u]["bias_n"])
            mx = cols[0]
            for c_ in cols[1:]:
                mx = jnp.maximum(mx, c_)
            m = jnp.maximum(jnp.max(mx, axis=-1, keepdims=True), sk)
            probs.append(jnp.concatenate([jnp.exp2(c_ - m).astype(BF16) for c_ in cols], axis=1))
            esks.append(jnp.exp2(sk - m))
        for (u, g), pr, esk in zip(items, probs, esks):
            pv = _dot(pr, keys_of(vv_scr if band else None, cvv_scr, blocks[u], g))
            res = pv[:, 0:LANES] / (pv[:, LANES:2 * LANES] + esk)
            for p in range(2):
                c0 = 128 * (2 * g + p)
                att_scr[u, :, c0:c0 + 128] = jnp.where(lo, res[2 * p * T:(2 * p + 1) * T],
                                                       res[(2 * p + 1) * T:(2 * p + 2) * T])
        for u in range(U):
            o_ref[0, pl.ds(blocks[u]["r0"], T), :] = (_rms(att_scr[u]) * g_ref[...]).astype(BF16)
        return carry

    lax.fori_loop(0, nb // U, step, 0)


def _attention(sink, p16, pc16, g_att, band):
    B, Lc, _ = pc16.shape
    L = p16.shape[1] if band else Lc
    qsrc = p16 if band else pc16
    kv_spec = lambda n: pl.BlockSpec((1, n, 256), lambda b: (b, 0, 4))
    in_specs = [pl.BlockSpec(memory_space=pltpu.SMEM),
                pl.BlockSpec((1, L, ATT_Q), lambda b: (b, 0, 1))]
    args = [sink, qsrc]
    scratch = []
    if band:
        in_specs.append(kv_spec(L))
        args.append(p16)
        scratch += [pltpu.VMEM((2, L, LANES), BF16), pltpu.VMEM((2, L, 2 * LANES), BF16)]
    in_specs += [kv_spec(Lc), _const_spec((1, ATT_Q))]
    args += [pc16, g_att]
    scratch += [pltpu.VMEM((2, Lc, LANES), BF16), pltpu.VMEM((2, Lc, 2 * LANES), BF16),
                pltpu.VMEM((ATT_UNROLL, ATT_BLOCK, ATT_Q), F32)]
    return pl.pallas_call(
        functools.partial(_att_kernel, L=L, Lc=Lc, band=band),
        grid=(B,),
        in_specs=in_specs,
        out_specs=pl.BlockSpec((1, L, ATT_Q), lambda b: (b, 0, 0)),
        out_shape=jax.ShapeDtypeStruct((B, L, ATT_Q), BF16),
        scratch_shapes=scratch,
        compiler_params=_cparams(("parallel",)),
        name="attn_window" if band else "attn_ctx",
    )(*args)


def _out_mlp_kernel(x_ref, nr_ref, nh_ref, na_ref, gta_ref, shm_ref, scm_ref, gtm_ref,
                    gpm_ref, gpre_ref, gpost_ref, wo_ref, w1_ref, w2_ref, o_ref, *, ff_chunk, sub):
    tm = x_ref.shape[1]
    dff = w1_ref.shape[1]

    def mixer_out(i):
        r = slice(i * sub, (i + 1) * sub)
        mix = (_dot(nr_ref[0, r, :], wo_ref[0:256, :]) + _dot(nh_ref[0, r, :], wo_ref[256:512, :])
               + _dot(na_ref[0, r, :], wo_ref[512:1024, :]))
        x1 = x_ref[0, r, :] + gta_ref[0] * (_rms(mix) * gpm_ref[...])
        h = ((_rms(x1) * gpre_ref[...]) * (1.0 + scm_ref[0]) + shm_ref[0]).astype(BF16)
        return x1, h

    def mlp(i, x1, h):
        acc = None
        for j in range(dff // ff_chunk):
            sl = slice(j * ff_chunk, (j + 1) * ff_chunk)
            hj = jnp.square(jnp.maximum(_dot(h, w1_ref[:, sl]), 0.0)).astype(BF16)
            part = _dot(hj, w2_ref[sl, :])
            acc = part if acc is None else acc + part
        o_ref[0, i * sub:(i + 1) * sub, :] = x1 + gtm_ref[0] * (_rms(acc) * gpost_ref[...])

    nxt = mixer_out(0)
    for i in range(tm // sub):
        cur = nxt
        if i + 1 < tm // sub:
            nxt = mixer_out(i + 1)
        mlp(i, *cur)


def _out_mlp(x3, nr, nh, na, mods, mod_row, g_post_mix, g_pre_mlp, g_post_mlp, wo16, w116, w216, l, tm):
    G, R, D = x3.shape
    row = lambda w: pl.BlockSpec((1, tm, w), lambda g, i: (g, i, 0))
    mod = lambda j: pl.BlockSpec((1, 1, D), lambda g, i: (mod_row(g), 0, j))
    return pl.pallas_call(
        functools.partial(_out_mlp_kernel, ff_chunk=1024, sub=min(512, tm)),
        grid=(G, R // tm),
        in_specs=[row(D), row(RET_V), row(HY_CH), row(ATT_Q), mod(2), mod(3), mod(4), mod(5),
                  _const_spec((1, D)), _const_spec((1, D)), _const_spec((1, D)),
                  _layer_spec(wo16, l), _layer_spec(w116, l), _layer_spec(w216, l)],
        out_specs=row(D),
        out_shape=jax.ShapeDtypeStruct((G, R, D), F32),
        compiler_params=_cparams(("parallel", "parallel")),
        name="out_mlp",
    )(x3, nr, nh, na, mods, mods, mods, mods, g_post_mix, g_pre_mlp, g_post_mlp, wo16, w116, w216)


def _rot_tables(cos, sin):
    lane = jnp.arange(LANES) % 32
    first = (lane < 16)[None, :]
    return cos, jnp.where(first, -sin, 0.0), jnp.where(first, 0.0, sin)


def _rotary_tables(L):
    pos = jnp.arange(L)
    row = (pos // GRID_W).astype(F32)
    col = (pos % GRID_W).astype(F32)
    half = ATT_HD // 2
    inv_ax = ROPE_BASE ** (-jnp.arange(0, half, 2, dtype=F32) / half)

    def tab(p, inv):
        ang = p[:, None] * inv[None, :]
        ang = jnp.concatenate([ang, ang], axis=-1)
        return jnp.cos(ang), jnp.sin(ang)

    cr, sr = tab(row, inv_ax)
    cc, sc = tab(col, inv_ax)
    inv_ret = 1.0 / (RET_ROT_BASE ** jnp.linspace(0.0, 1.0, RET_DK // 2, dtype=F32))
    ct, st = tab(pos.astype(F32), inv_ret)
    ret = _rot_tables(jnp.tile(ct, (1, 4)), jnp.tile(st, (1, 4)))
    ax = _rot_tables(jnp.tile(jnp.concatenate([cr, cc], -1), (1, 2)),
                     jnp.tile(jnp.concatenate([sr, sc], -1), (1, 2)))
    return ret + ax


def _dft_mats(L):
    n = 2 * L
    k = jnp.arange(L // 2, dtype=jnp.int32)

    def pair(t):
        ang = ((k[:, None] * t[None, :]) % n).astype(F32) * (2.0 * math.pi / n)
        return jnp.cos(ang).astype(BF16), (-jnp.sin(ang)).astype(BF16)

    ce, se = pair(2 * k)
    co, so = pair(2 * k + 1)
    return ce, se, co, so, co.T, so.T


def _filter_features(L):
    t = jnp.linspace(0.0, 1.0, L, dtype=F32)[:, None]
    w = 2.0 * math.pi * jnp.arange(L, dtype=F32) / L
    bands = jnp.linspace(1e-4, HY_BANDS - 1, HY_BANDS, dtype=F32)
    ang = w[:, None] * bands[None, :]
    z = jnp.concatenate([t, jnp.cos(ang), -jnp.sin(ang)], axis=-1)
    zf = jnp.pad(z, ((0, 0), (0, LANES - HY_EMB)))
    max_decay = math.log(HY_TARGET) / HY_FAST_DECAY
    min_decay = math.log(HY_TARGET) / HY_SLOW_DECAY
    deltas = jnp.linspace(min_decay, max_decay, HY_CH, dtype=F32)
    dec = jnp.exp(-t * jnp.abs(deltas)[None, :])
    return zf, dec


def _row_tile(rows):
    return 1024 if rows % 1024 == 0 else 512


def _pad_to(a, shape):
    return jnp.pad(a, [(0, s - d) for d, s in zip(a.shape, shape)])


def kernel(x, c, ctx, c_ctx, w_ada, b_ada, g_pre_mix, g_post_mix, g_pre_mlp, g_post_mlp, w_in,
           ret_decay_fwd, ret_decay_bwd, hy_short_w, hy_short_b, hy_f_w1, hy_f_b1, hy_f_w2, hy_f_b2,
           hy_f_w3, hy_f_freq, hy_bias, attn_sink, g_ret, g_hy, g_att, w_out, w_ff1, w_ff2):
    B, L, D = x.shape
    Lc = ctx.shape[1]
    depth = w_ada.shape[0]
    assert D == 1024 and w_in.shape[2] == 2304 and L % 256 == 0 and Lc % 256 == 0 and L >= 3 * ATT_BLOCK

    rows = -(-(B + 1) // 8) * 8
    cc = _pad_to(jnp.concatenate([c, c_ctx[None, :]], axis=0), (rows, D))
    mods_all = _ada(cc, w_ada, b_ada).reshape(depth, rows, 1, 6 * D)

    w_in16 = w_in.astype(BF16)
    w_out16 = w_out.astype(BF16)
    w_ff116 = w_ff1.astype(BF16)
    w_ff216 = w_ff2.astype(BF16)

    tabs = _rotary_tables(L)
    mats = {n: _dft_mats(n) for n in {L, Lc}}
    feats = {n: _filter_features(n) for n in {L, Lc}}

    ctx_rows = B * Lc
    tm = _row_tile(L)
    tc = _row_tile(ctx_rows)
    xc = ctx.reshape(ctx_rows // tc, tc, D)
    lat_row = lambda g: g
    ctx_row = lambda g: B

    for l in range(depth):
        last = l == depth - 1
        mods = mods_all[l]
        dec2 = jnp.stack([ret_decay_fwd[l], ret_decay_bwd[l]])
        dq = jnp.repeat(dec2, RET_DK, axis=1)
        ds = jnp.repeat(dec2, RET_CHUNK, axis=1)
        dv = jnp.repeat(dec2, RET_DV, axis=1)
        gr, gh, ga = g_ret[l][None], g_hy[l][None], g_att[l][None]
        gpre, gpm = g_pre_mix[l][None], g_post_mix[l][None]
        gprm, gpom = g_pre_mlp[l][None], g_post_mlp[l][None]
        fw1 = _pad_to(hy_f_w1[l], (LANES, LANES))
        fb1 = _pad_to(hy_f_b1[l][None], (1, LANES))
        fw2 = _pad_to(hy_f_w2[l], (2, LANES, LANES))
        fb2 = _pad_to(hy_f_b2[l][:, None, :], (2, 1, LANES))
        fw3 = _pad_to(hy_f_w3[l], (LANES, 2 * HY_CH))
        ffr = _pad_to(hy_f_freq[l][None], (1, LANES))
        sw, sb, hbias = hy_short_w[l], hy_short_b[l][None], hy_bias[l][None]

        def spectrum(n):
            zf, dec = feats[n]
            return _hyena_spectrum(n, zf, fw1, fb1, fw2, fb2, fw3, ffr, dec, mats[n])

        pc16, pc32 = _in_proj(xc, mods, ctx_row, gpre, w_in16, l, None, tc)
        pc16 = pc16.reshape(B, Lc, P16_W)
        pc32 = pc32.reshape(B, Lc, P32_W)
        cret, cstate = _retention(pc16, pc32, dq, ds, dv, gr, None)

        p16, p32 = _in_proj(x, mods, lat_row, gpre, w_in16, l, tabs, tm)
        ret, _ = _retention(p16, p32, dq, ds, dv, gr, cstate)
        hyo = _hyena(p32, sw, sb, hbias, gh, mats[L], spectrum(L))
        att = _attention(attn_sink[l], p16, pc16, ga, True)
        x = _out_mlp(x, ret, hyo, att, mods, lat_row, gpm, gprm, gpom,
                     w_out16, w_ff116, w_ff216, l, tm)

        if not last:
            chyo = _hyena(pc32, sw, sb, hbias, gh, mats[Lc], spectrum(Lc))
            catt = _attention(attn_sink[l], None, pc16, ga, False)
            r3 = lambda a: a.reshape(ctx_rows // tc, tc, a.shape[-1])
            xc = _out_mlp(xc, r3(cret), r3(chyo), r3(catt), mods, ctx_row, gpm, gprm, gpom,
                          w_out16, w_ff116, w_ff216, l, tc)
    return x
```

```python
import functools
import math

import jax
import jax.numpy as jnp
from jax import lax
from jax.experimental import pallas as pl
from jax.experimental.pallas import tpu as pltpu

F32 = jnp.float32
BF16 = jnp.bfloat16
EPS = 1e-6
NEG_INF = -1e30
LOG2E = math.log2(math.e)

RET_HEADS = 4
RET_DK = 32
RET_DV = 64
RET_QK = RET_HEADS * RET_DK
RET_V = RET_HEADS * RET_DV
RET_CHUNK = 128
RET_UNROLL = 4
RET_ROT_BASE = 10000.0
HY_CH = 256
HY_EMB = 33
HY_BANDS = 16
HY_ORDER = 64
HY_FAST_DECAY = 0.3
HY_SLOW_DECAY = 1.5
HY_TARGET = 1e-2
ATT_HEADS = 8
ATT_KV_HEADS = 2
ATT_HD = 64
ATT_Q = ATT_HEADS * ATT_HD
ATT_KV = ATT_KV_HEADS * ATT_HD
WINDOW = 128
ATT_BLOCK = 128
ATT_UNROLL = 8
ROPE_BASE = 10000.0
GRID_W = 64

P16_W = 2 * RET_QK + RET_V + ATT_Q + 2 * ATT_KV
P32_W = RET_V + 3 * HY_CH

LANES = 128
VMEM_LIMIT = 56 * 1024 * 1024


def _cparams(sem):
    return pltpu.CompilerParams(dimension_semantics=sem, vmem_limit_bytes=VMEM_LIMIT)


def _const_spec(shape):
    nd = len(shape)
    return pl.BlockSpec(shape, lambda *_: (0,) * nd, pipeline_mode=pl.Buffered(1))


def _layer_spec(stacked, l):
    nd = stacked.ndim - 1
    return pl.BlockSpec((None,) + stacked.shape[1:], lambda *_: (l,) + (0,) * nd,
                        pipeline_mode=pl.Buffered(1))


def _rms(x):
    return x * lax.rsqrt(jnp.mean(x * x, axis=-1, keepdims=True) + EPS)


def _dot(a, b):
    return jnp.dot(a, b, preferred_element_type=F32)


def _dot_nt(a, b):
    return lax.dot_general(a, b, (((1,), (1,)), ((), ())), preferred_element_type=F32)


def _dot_tn(a, b):
    return lax.dot_general(a, b, (((0,), (0,)), ((), ())), preferred_element_type=F32)


def _split(a):
    hi = a.astype(BF16)
    lo = (a - hi.astype(F32)).astype(BF16)
    return hi, lo


def _dot3(a, b):
    ah, al = _split(a)
    bh, bl = _split(b)
    return _dot(ah, bh) + _dot(al, bh) + _dot(ah, bl)


def _ada_kernel(c_ref, w_ref, b_ref, o_ref):
    cv = c_ref[...]
    s = cv * jax.nn.sigmoid(cv)
    o_ref[0] = _dot(s.astype(BF16), w_ref[0].astype(BF16)) + b_ref[0]


def _ada(cc, w_ada, b_ada):
    depth, d, d6 = w_ada.shape
    rows = cc.shape[0]
    tn = 1536
    return pl.pallas_call(
        _ada_kernel,
        grid=(depth, d6 // tn),
        in_specs=[
            pl.BlockSpec((rows, d), lambda l, j: (0, 0)),
            pl.BlockSpec((1, d, tn), lambda l, j: (l, 0, j)),
            pl.BlockSpec((1, 1, tn), lambda l, j: (l, 0, j)),
        ],
        out_specs=pl.BlockSpec((1, rows, tn), lambda l, j: (l, 0, j)),
        out_shape=jax.ShapeDtypeStruct((depth, rows, d6), F32),
        compiler_params=_cparams(("arbitrary", "arbitrary")),
        name="ada",
    )(cc, w_ada, b_ada.reshape(depth, 1, d6))


def _rot(x, cos, sa, sb):
    return x * cos + pltpu.roll(x, LANES - 16, 1) * sa + pltpu.roll(x, 16, 1) * sb


def _in_kernel(x_ref, sh_ref, sc_ref, g_ref, w_ref, *rest, rotary, sub):
    if rotary:
        rc, rsa, rsb, ac, asa, asb, o16_ref, o32_ref = rest
    else:
        o16_ref, o32_ref = rest
    tm = x_ref.shape[1]
    k_scale = RET_DK ** -0.5
    q_scale = ATT_HD ** -0.5 * LOG2E

    def normed(i):
        x = x_ref[0, i * sub:(i + 1) * sub, :]
        return ((_rms(x) * g_ref[...]) * (1.0 + sc_ref[0]) + sh_ref[0]).astype(BF16)

    def project(i, hb):
        r = slice(i * sub, (i + 1) * sub)

        def rot(v, tabs):
            return _rot(v, *[t[r, :] for t in tabs]) if rotary else v

        ret_tabs = (rc, rsa, rsb) if rotary else None
        ax_tabs = (ac, asa, asb) if rotary else None
        pr = _dot(hb, w_ref[:, 0:512])
        o16_ref[0, r, 0:128] = rot(pr[:, 0:128], ret_tabs).astype(BF16)
        o16_ref[0, r, 128:256] = (rot(pr[:, 128:256], ret_tabs) * k_scale).astype(BF16)
        o16_ref[0, r, 256:512] = pr[:, 256:512].astype(BF16)
        o32_ref[0, r, :] = _dot(hb, w_ref[:, 512:1536])
        pa = _dot(hb, w_ref[:, 1536:2304])
        for j in range(4):
            aq = rot(pa[:, 128 * j:128 * (j + 1)], ax_tabs)
            o16_ref[0, r, 512 + 128 * j:640 + 128 * j] = (aq * q_scale).astype(BF16)
        o16_ref[0, r, 1024:1152] = rot(pa[:, 512:640], ax_tabs).astype(BF16)
        o16_ref[0, r, 1152:1280] = pa[:, 640:768].astype(BF16)

    nxt = normed(0)
    for i in range(tm // sub):
        cur = nxt
        if i + 1 < tm // sub:
            nxt = normed(i + 1)
        project(i, cur)


def _in_proj(x3, mods, mod_row, g_pre, w_in16, l, tabs, tm):
    G, R, D = x3.shape
    rotary = tabs is not None
    nt = R // tm
    in_specs = [
        pl.BlockSpec((1, tm, D), lambda g, i: (g, i, 0)),
        pl.BlockSpec((1, 1, D), lambda g, i: (mod_row(g), 0, 0)),
        pl.BlockSpec((1, 1, D), lambda g, i: (mod_row(g), 0, 1)),
        _const_spec((1, D)),
        _layer_spec(w_in16, l),
    ]
    args = [x3, mods, mods, g_pre, w_in16]
    if rotary:
        in_specs += [pl.BlockSpec((tm, LANES), lambda g, i: (i, 0))] * 6
        args += list(tabs)
    return pl.pallas_call(
        functools.partial(_in_kernel, rotary=rotary, sub=min(256, tm)),
        grid=(G, nt),
        in_specs=in_specs,
        out_specs=[
            pl.BlockSpec((1, tm, P16_W), lambda g, i: (g, i, 0)),
            pl.BlockSpec((1, tm, P32_W), lambda g, i: (g, i, 0)),
        ],
        out_shape=[
            jax.ShapeDtypeStruct((G, R, P16_W), BF16),
            jax.ShapeDtypeStruct((G, R, P32_W), F32),
        ],
        compiler_params=_cparams(("parallel", "parallel")),
        name="in_proj_rot" if rotary else "in_proj",
    )(*args)


def _ret_kernel(qkv_ref, rg_ref, dq_ref, ds_ref, dv_ref, g_ref, *rest, L, has_init, want_out):
    rest = list(rest)
    s0_ref = rest.pop(0) if has_init else None
    o_ref = rest.pop(0) if want_out else None
    sfin_ref, st_scr, dec_scr, dmask_scr, cdec_scr = rest
    C = RET_CHUNK
    N = L // C

    @pl.when(pl.program_id(0) == 0)
    def _():
        def log_gamma(ref, r):
            return jnp.log1p(-jnp.exp(ref[r:r + 1, :]))

        lfq, lbq = log_gamma(dq_ref, 0), log_gamma(dq_ref, 1)
        lfs, lbs = log_gamma(ds_ref, 0), log_gamma(ds_ref, 1)
        ri = lax.broadcasted_iota(jnp.int32, (C, LANES), 0).astype(F32)
        dec_scr[0] = jnp.exp(lfq * (ri + 1.0))
        dec_scr[1] = jnp.exp(lfq * (C - 1.0 - ri))
        dec_scr[2] = jnp.exp(lbq * (C - ri))
        dec_scr[3] = jnp.exp(lbq * ri)
        di = lax.broadcasted_iota(jnp.int32, (C, 4 * C), 0)
        dj = lax.broadcasted_iota(jnp.int32, (C, 4 * C), 1) & (C - 1)
        diff = (di - dj).astype(F32)
        dmask_scr[...] = (jnp.where(diff >= 0, jnp.exp(lfs * jnp.maximum(diff, 0.0)), 0.0)
                          + jnp.where(diff <= 0, jnp.exp(lbs * jnp.maximum(-diff, 0.0)), 0.0))
        cdec_scr[0:1, :] = jnp.exp(log_gamma(dv_ref, 0) * float(C))
        cdec_scr[1:2, :] = jnp.exp(log_gamma(dv_ref, 1) * float(C))

    cdec_f, cdec_b = cdec_scr[0:1, :], cdec_scr[1:2, :]

    lane_q = lax.broadcasted_iota(jnp.int32, (1, RET_QK), 1) >> 5
    lane_v = lax.broadcasted_iota(jnp.int32, (1, RET_V), 1) >> 6
    hm = [jnp.where(lane_q == h, 1.0, 0.0).astype(BF16) for h in range(RET_HEADS)]
    cm = [jnp.where(lane_v == h, 1.0, 0.0).astype(BF16) for h in range(RET_HEADS)]
    bd = ((lax.broadcasted_iota(jnp.int32, (RET_QK, RET_V), 0) >> 5)
          == (lax.broadcasted_iota(jnp.int32, (RET_QK, RET_V), 1) >> 6))
    ones64 = jnp.where((lax.broadcasted_iota(jnp.int32, (RET_V, RET_V), 0) >> 6)
                       == (lax.broadcasted_iota(jnp.int32, (RET_V, RET_V), 1) >> 6), 1.0, 0.0).astype(BF16)

    if has_init:
        sf0 = s0_ref[0, 0]
        sb0 = s0_ref[0, 1]
    else:
        sf0 = jnp.zeros((RET_QK, RET_V), F32)
        sb0 = jnp.zeros((RET_QK, RET_V), F32)

    def kv_update(k16, v16, kdec, cdec, s):
        kd = (k16.astype(F32) * kdec).astype(BF16)
        return cdec * s + jnp.where(bd, _dot_tn(kd, v16), 0.0)

    def kv_at(n):
        r0 = pl.multiple_of(n * C, C)
        return qkv_ref[0, pl.ds(r0, C), 128:256], qkv_ref[0, pl.ds(r0, C), 256:512]

    U = min(RET_UNROLL, N)

    def scan_body(it, carry):
        sf, sb = carry
        for u in range(U):
            nf = it * U + u
            nb = N - 1 - nf
            st_scr[nf, 0:RET_QK, :] = sf.astype(BF16)
            sf = kv_update(*kv_at(nf), dec_scr[1], cdec_f, sf)
            st_scr[nb, RET_QK:2 * RET_QK, :] = sb.astype(BF16)
            sb = kv_update(*kv_at(nb), dec_scr[3], cdec_b, sb)
        return sf, sb

    sf_fin, sb_fin = lax.fori_loop(0, N // U, scan_body, (sf0, sb0))
    sfin_ref[0, 0] = sf_fin
    sfin_ref[0, 1] = sb_fin

    if not want_out:
        return

    def out_body(it, carry):
        ns = [it * U + u for u in range(U)]
        rows = [pl.ds(pl.multiple_of(n * C, C), C) for n in ns]
        q16s = [qkv_ref[0, r, 0:128] for r in rows]
        kvs = [kv_at(n) for n in ns]
        scores = [_dot_nt(q16, jnp.concatenate([k16 * hm[h] for h in range(RET_HEADS)], axis=0))
                  for q16, (k16, _) in zip(q16s, kvs)]
        outs = []
        for n, q16, (_, v16), s in zip(ns, q16s, kvs, scores):
            qf = q16.astype(F32)
            vblk = jnp.concatenate([v16 * cm[h] for h in range(RET_HEADS)], axis=0)
            lhs = jnp.concatenate([(s * dmask_scr[...]).astype(BF16), (qf * dec_scr[0]).astype(BF16),
                                   (qf * dec_scr[2]).astype(BF16)], axis=1)
            outs.append(_dot(lhs, jnp.concatenate([vblk, st_scr[n]], axis=0)))
        mss = [_dot((o * o).astype(BF16), ones64) * (1.0 / RET_DV) for o in outs]
        for r, o, ms in zip(rows, outs, mss):
            rg = rg_ref[0, r, :]
            gated = (o * lax.rsqrt(ms + EPS)) * (rg * jax.nn.sigmoid(rg))
            o_ref[0, r, :] = (_rms(gated) * g_ref[...]).astype(BF16)
        return carry

    lax.fori_loop(0, N // U, out_body, 0)


def _retention(p16, p32, dq, ds, dv, g_ret, s0, want_out=True):
    B, L, _ = p16.shape
    has_init = s0 is not None
    in_specs = [
        pl.BlockSpec((1, L, 512), lambda b: (b, 0, 0)),
        pl.BlockSpec((1, L, RET_V), lambda b: (b, 0, 0)),
        _const_spec(dq.shape), _const_spec(ds.shape), _const_spec(dv.shape),
        _const_spec((1, RET_V)),
    ]
    args = [p16, p32, dq, ds, dv, g_ret]
    if has_init:
        in_specs.append(pl.BlockSpec((1, 2, RET_QK, RET_V), lambda b: (b, 0, 0, 0)))
        args.append(s0)
    out_specs = [pl.BlockSpec((1, 2, RET_QK, RET_V), lambda b: (b, 0, 0, 0))]
    out_shape = [jax.ShapeDtypeStruct((B, 2, RET_QK, RET_V), F32)]
    if want_out:
        out_specs.insert(0, pl.BlockSpec((1, L, RET_V), lambda b: (b, 0, 0)))
        out_shape.insert(0, jax.ShapeDtypeStruct((B, L, RET_V), BF16))
    res = pl.pallas_call(
        functools.partial(_ret_kernel, L=L, has_init=has_init, want_out=want_out),
        grid=(B,),
        in_specs=in_specs,
        out_specs=out_specs,
        out_shape=out_shape,
        scratch_shapes=[pltpu.VMEM((L // RET_CHUNK, 2 * RET_QK, RET_V), BF16),
                        pltpu.VMEM((4, RET_CHUNK, LANES), F32),
                        pltpu.VMEM((RET_CHUNK, 4 * RET_CHUNK), F32),
                        pltpu.VMEM((2, RET_V), F32)],
        compiler_params=_cparams(("arbitrary",)),
        name="retention_init" if has_init else "retention",
    )(*args)
    return tuple(res) if want_out else (None, res[0])


def _halves(scr, start, rows):
    return jnp.concatenate([scr[0, pl.ds(start, rows, stride=2), :],
                            scr[1, pl.ds(start, rows, stride=2), :]], axis=1)


def _alt_sign(rows, first=0):
    r = lax.broadcasted_iota(jnp.int32, (rows, HY_CH), 0) + first
    return jnp.where((r & 1) == 1, -1.0, 1.0)


def _filt_kernel(zf_ref, w1_ref, b1_ref, w2_ref, b2_ref, w3_ref, fr_ref, dec_ref,
                 ce_ref, se_ref, co_ref, so_ref, hra_ref, hia_ref, hrb_ref, hib_ref, hm_ref,
                 a_scr, d_scr, *, L):
    n = 2 * L
    M = L // 2
    fr = fr_ref[...]
    h = jnp.sin(fr * (_dot3(zf_ref[...], w1_ref[...]) + b1_ref[...]))
    for j in range(2):
        h = jnp.sin(fr * (_dot3(h, w2_ref[j]) + b2_ref[j]))
    h = _dot3(h, w3_ref[...])
    dec = dec_ref[...]
    row = lax.broadcasted_iota(jnp.int32, (L, HY_CH), 0)
    hf = h[:, 0:HY_CH] * dec
    hb = jnp.where(row == 0, 0.0, h[:, HY_CH:2 * HY_CH] * dec)
    for scr, val in ((a_scr, hf + hb), (d_scr, hf - hb)):
        scr[0] = val[:, 0:LANES]
        scr[1] = val[:, LANES:2 * LANES]
    ae, ao = _halves(a_scr, 0, M), _halves(a_scr, 1, M)
    de, do = _halves(d_scr, 0, M), _halves(d_scr, 1, M)
    sgn = _alt_sign(M)
    hm_ref[0:1, :] = jnp.sum(ae * sgn, axis=0, keepdims=True) * (2.0 / n)
    hm_ref[1:2, :] = jnp.sum(do * sgn, axis=0, keepdims=True) * (-2.0 / n)

    def dft(m_ref, x):
        xh, xl = _split(x)
        return _dot(m_ref[...], xh) + _dot(m_ref[...], xl)

    wk = jnp.where(lax.broadcasted_iota(jnp.int32, (M, 1), 0) == 0, 1.0 / n, 2.0 / n)
    ea, ta = dft(ce_ref, ae), dft(co_ref, ao)
    hra_ref[...] = (ea + ta) * wk
    hrb_ref[...] = (ea - ta) * wk
    ed, td = dft(se_ref, de), dft(so_ref, do)
    hia_ref[...] = (ed + td) * wk
    hib_ref[...] = (td - ed) * wk


def _hyena_spectrum(L, zf, w1, b1, w2, b2, w3, fr, dec, mats):
    M = L // 2
    args = (zf, w1, b1, w2, b2, w3, fr, dec) + tuple(mats[:4])
    half = jax.ShapeDtypeStruct((M, HY_CH), F32)
    return pl.pallas_call(
        functools.partial(_filt_kernel, L=L),
        grid=(1,),
        in_specs=[_const_spec(a.shape) for a in args],
        out_specs=[_const_spec((M, HY_CH))] * 4 + [_const_spec((2, HY_CH))],
        out_shape=[half] * 4 + [jax.ShapeDtypeStruct((2, HY_CH), F32)],
        scratch_shapes=[pltpu.VMEM((2, L, LANES), F32)] * 2,
        compiler_params=_cparams(("arbitrary",)),
        name="hyena_filter",
    )(*args)


def _hy_kernel(v_ref, x1_ref, x0_ref, sw_ref, sb_ref, bias_ref, g_ref,
               ce_ref, se_ref, co_ref, so_ref, cot_ref, sot_ref,
               hra_ref, hia_ref, hrb_ref, hib_ref, hm_ref, o_ref,
               z_scr, x0_scr, out_scr, ger_scr, gei_scr, gor_scr, goi_scr, *, L, R):
    nchunks = L // R
    M = L // 2
    RB = min(256, M)
    for c in range(nchunks):
        r0 = c * R
        lo = max(r0 - 8, 0)
        hi = min(r0 + R + 8, L)
        rows = hi - lo
        off = r0 - lo
        grow = lax.broadcasted_iota(jnp.int32, (R, HY_CH), 0) + r0

        def conv(ref, c0):
            ext = ref[0, lo:hi, :]
            up = pltpu.roll(ext, 1, 0)[off:off + R]
            un = pltpu.roll(ext, rows - 1, 0)[off:off + R]
            if c == 0:
                up = jnp.where(grow == 0, 0.0, up)
            if c == nchunks - 1:
                un = jnp.where(grow == L - 1, 0.0, un)
            u = ref[0, r0:r0 + R, :]
            return (up * sw_ref[0:1, c0:c0 + HY_CH] + u * sw_ref[1:2, c0:c0 + HY_CH]
                    + un * sw_ref[2:3, c0:c0 + HY_CH] + sb_ref[:, c0:c0 + HY_CH])

        z = conv(v_ref, 0) * conv(x1_ref, HY_CH)
        x0 = conv(x0_ref, 2 * HY_CH)
        for hlf in range(2):
            z_scr[hlf, r0:r0 + R, :] = z[:, hlf * LANES:(hlf + 1) * LANES]
            x0_scr[hlf, r0:r0 + R, :] = x0[:, hlf * LANES:(hlf + 1) * LANES]

    ze, zo = _halves(z_scr, 0, M), _halves(z_scr, 1, M)
    sgn = _alt_sign(M)
    mid_r = jnp.sum(ze * sgn, axis=0, keepdims=True)
    mid_i = -jnp.sum(zo * sgn, axis=0, keepdims=True)
    ym_r = mid_r * hm_ref[0:1, :] - mid_i * hm_ref[1:2, :]
    ym_i = mid_r * hm_ref[1:2, :] + mid_i * hm_ref[0:1, :]
    zeb, zob = ze.astype(BF16), zo.astype(BF16)
    for kb in range(M // RB):
        ks = slice(kb * RB, (kb + 1) * RB)
        er, ei = _dot(ce_ref[ks, :], zeb), _dot(se_ref[ks, :], zeb)
        tr, ti = _dot(co_ref[ks, :], zob), _dot(so_ref[ks, :], zob)
        ar, ai, br, bi = er + tr, ei + ti, er - tr, ti - ei
        hra, hia, hrb, hib = hra_ref[ks, :], hia_ref[ks, :], hrb_ref[ks, :], hib_ref[ks, :]
        yar, yai = ar * hra - ai * hia, ar * hia + ai * hra
        ybr, ybi = br * hrb - bi * hib, br * hib + bi * hrb
        ger_scr[ks, :] = (yar + ybr).astype(BF16)
        gei_scr[ks, :] = (yai - ybi).astype(BF16)
        gor_scr[ks, :] = (yar - ybr).astype(BF16)
        goi_scr[ks, :] = (yai + ybi).astype(BF16)

    ger, gei, gor, goi = ger_scr[...], gei_scr[...], gor_scr[...], goi_scr[...]
    for tb in range(M // RB):
        ts = slice(tb * RB, (tb + 1) * RB)
        sg = _alt_sign(RB, tb * RB)
        ys = (_dot(ce_ref[ts, :], ger) + _dot(se_ref[ts, :], gei) + sg * ym_r,
              _dot(cot_ref[ts, :], gor) + _dot(sot_ref[ts, :], goi) - sg * ym_i)
        for par, y in enumerate(ys):
            first = 2 * tb * RB + par
            out = (y + _halves(z_scr, first, RB) * bias_ref[...]) * _halves(x0_scr, first, RB)
            out = _rms(out) * g_ref[...]
            for hlf in range(2):
                out_scr[hlf, pl.ds(first, RB, stride=2), :] = out[:, hlf * LANES:(hlf + 1) * LANES]

    for c in range(nchunks):
        r = slice(c * R, (c + 1) * R)
        o_ref[0, r, :] = jnp.concatenate([out_scr[0, r, :], out_scr[1, r, :]], axis=1).astype(BF16)


def _hyena(p32, short_w, short_b, hy_bias, g_hy, mats, spec):
    B, L, _ = p32.shape
    R = min(256, L)
    M = L // 2
    col = lambda j: pl.BlockSpec((1, L, HY_CH), lambda b: (b, 0, j))
    consts = (short_w, short_b, hy_bias, g_hy) + tuple(mats) + tuple(spec)
    return pl.pallas_call(
        functools.partial(_hy_kernel, L=L, R=R),
        grid=(B,),
        in_specs=[col(1), col(2), col(3)] + [_const_spec(a.shape) for a in consts],
        out_specs=pl.BlockSpec((1, L, HY_CH), lambda b: (b, 0, 0)),
        out_shape=jax.ShapeDtypeStruct((B, L, HY_CH), BF16),
        scratch_shapes=[pltpu.VMEM((2, L, LANES), F32)] * 3 + [pltpu.VMEM((M, HY_CH), BF16)] * 4,
        compiler_params=_cparams(("parallel",)),
        name="hyena",
    )(p32, p32, p32, *consts)


def _expand_kv(kv_ref, k2_scr, vv_scr, rows):
    R = min(256, rows)
    lo = lax.broadcasted_iota(jnp.int32, (R, LANES), 1) < ATT_HD
    ones = jnp.ones((R, LANES), BF16)
    for c in range(rows // R):
        sl = slice(c * R, (c + 1) * R)
        k = kv_ref[0, sl, 0:128].astype(F32)
        v = kv_ref[0, sl, 128:256].astype(F32)
        kr = pltpu.roll(k, ATT_HD, 1)
        vr = pltpu.roll(v, ATT_HD, 1)
        k2_scr[0, sl, :] = jnp.where(lo, k, kr).astype(BF16)
        k2_scr[1, sl, :] = jnp.where(lo, kr, k).astype(BF16)
        vv_scr[0, sl, 0:128] = jnp.where(lo, v, vr).astype(BF16)
        vv_scr[1, sl, 0:128] = jnp.where(lo, vr, v).astype(BF16)
        vv_scr[0, sl, 128:256] = ones
        vv_scr[1, sl, 128:256] = ones


def _att_kernel(sink_ref, q_ref, *rest, L, Lc, band):
    if band:
        (kv_ref, ckv_ref, g_ref, o_ref, k2_scr, vv_scr, ck2_scr, cvv_scr, att_scr) = rest
        _expand_kv(kv_ref, k2_scr, vv_scr, L)
    else:
        (ckv_ref, g_ref, o_ref, ck2_scr, cvv_scr, att_scr) = rest
    _expand_kv(ckv_ref, ck2_scr, cvv_scr, Lc)

    T = ATT_BLOCK
    nb = L // T
    lane = lax.broadcasted_iota(jnp.int32, (1, LANES), 1)
    lo = lane < ATT_HD
    lo16 = jnp.where(lo, 1.0, 0.0).astype(BF16)
    hi16 = jnp.where(lo, 0.0, 1.0).astype(BF16)
    hrow = lax.broadcasted_iota(jnp.int32, (4 * T, 1), 0) >> 7
    ti = lax.broadcasted_iota(jnp.int32, (T, T), 0)
    tj = lax.broadcasted_iota(jnp.int32, (T, T), 1)
    tri_prev = jnp.where(tj >= ti, 0.0, NEG_INF)
    tri_next = jnp.where(tj <= ti, 0.0, NEG_INF)

    def add_bias(s, bias):
        return (s.reshape(4, T, T) + bias[None]).reshape(4 * T, T)

    U = min(ATT_UNROLL, nb)

    def step(it, carry):
        blocks = []
        for u in range(U):
            n = it * U + u
            blk = dict(r0=pl.multiple_of(n * T, T))
            if band:
                blk.update(rp=pl.multiple_of(jnp.maximum(n - 1, 0) * T, T),
                           rn=pl.multiple_of(jnp.minimum(n + 1, nb - 1) * T, T),
                           bias_p=tri_prev + jnp.where(n == 0, NEG_INF, 0.0),
                           bias_n=tri_next + jnp.where(n == nb - 1, NEG_INF, 0.0))
            blocks.append(blk)
        items = [(u, g) for u in range(U) for g in range(ATT_KV_HEADS)]

        def keys_of(scr, cscr, blk, g):
            parts = [cscr[g]]
            if band:
                parts += [scr[g, pl.ds(blk[k], T), :] for k in ("rp", "r0", "rn")]
            return jnp.concatenate(parts, axis=0)

        scores, sinks = [], []
        for u, g in items:
            qb = q_ref[0, pl.ds(blocks[u]["r0"], T), 256 * g:256 * (g + 1)]
            qs = []
            for p in range(2):
                qp = qb[:, 128 * p:128 * (p + 1)]
                qs += [qp * lo16, qp * hi16]
            qst = jnp.concatenate(qs, axis=0)
            scores.append(_dot_nt(qst, keys_of(k2_scr if band else None, ck2_scr, blocks[u], g)))
            sinks.append(LOG2E * jnp.where(
                hrow == 0, sink_ref[4 * g],
                jnp.where(hrow == 1, sink_ref[4 * g + 1],
                          jnp.where(hrow == 2, sink_ref[4 * g + 2], sink_ref[4 * g + 3]))))
        probs, esks = [], []
        for (u, g), s, sk in zip(items, scores, sinks):
            cols = [s[:, j * T:(j + 1) * T] for j in range(s.shape[1] // T)]
            if band:
                cols[-3] = add_bias(cols[-3], blocks[u]["bias_p"])
                cols[-1] = add_bias(cols[-1], blocks[u]["bias_n"])
            mx = cols[0]
            for c_ in cols[1:]:
                mx = jnp.maximum(mx, c_)
            m = jnp.maximum(jnp.max(mx, axis=-1, keepdims=True), sk)
            probs.append(jnp.concatenate([jnp.exp2(c_ - m).astype(BF16) for c_ in cols], axis=1))
            esks.append(jnp.exp2(sk - m))
        for (u, g), pr, esk in zip(items, probs, esks):
            pv = _dot(pr, keys_of(vv_scr if band else None, cvv_scr, blocks[u], g))
            res = pv[:, 0:LANES] / (pv[:, LANES:2 * LANES] + esk)
            for p in range(2):
                c0 = 128 * (2 * g + p)
                att_scr[u, :, c0:c0 + 128] = jnp.where(lo, res[2 * p * T:(2 * p + 1) * T],
                                                       res[(2 * p + 1) * T:(2 * p + 2) * T])
        for u in range(U):
            o_ref[0, pl.ds(blocks[u]["r0"], T), :] = (_rms(att_scr[u]) * g_ref[...]).astype(BF16)
        return carry

    lax.fori_loop(0, nb // U, step, 0)


def _attention(sink, p16, pc16, g_att, band):
    B, Lc, _ = pc16.shape
    L = p16.shape[1] if band else Lc
    qsrc = p16 if band else pc16
    kv_spec = lambda n: pl.BlockSpec((1, n, 256), lambda b: (b, 0, 4))
    in_specs = [pl.BlockSpec(memory_space=pltpu.SMEM),
                pl.BlockSpec((1, L, ATT_Q), lambda b: (b, 0, 1))]
    args = [sink, qsrc]
    scratch = []
    if band:
        in_specs.append(kv_spec(L))
        args.append(p16)
        scratch += [pltpu.VMEM((2, L, LANES), BF16), pltpu.VMEM((2, L, 2 * LANES), BF16)]
    in_specs += [kv_spec(Lc), _const_spec((1, ATT_Q))]
    args += [pc16, g_att]
    scratch += [pltpu.VMEM((2, Lc, LANES), BF16), pltpu.VMEM((2, Lc, 2 * LANES), BF16),
                pltpu.VMEM((ATT_UNROLL, ATT_BLOCK, ATT_Q), F32)]
    return pl.pallas_call(
        functools.partial(_att_kernel, L=L, Lc=Lc, band=band),
        grid=(B,),
        in_specs=in_specs,
        out_specs=pl.BlockSpec((1, L, ATT_Q), lambda b: (b, 0, 0)),
        out_shape=jax.ShapeDtypeStruct((B, L, ATT_Q), BF16),
        scratch_shapes=scratch,
        compiler_params=_cparams(("parallel",)),
        name="attn_window" if band else "attn_ctx",
    )(*args)


def _out_mlp_kernel(x_ref, nr_ref, nh_ref, na_ref, gta_ref, shm_ref, scm_ref, gtm_ref,
                    gpm_ref, gpre_ref, gpost_ref, wo_ref, w1_ref, w2_ref, o_ref, *, ff_chunk, sub):
    tm = x_ref.shape[1]
    dff = w1_ref.shape[1]

    def mixer_out(i):
        r = slice(i * sub, (i + 1) * sub)
        mix = (_dot(nr_ref[0, r, :], wo_ref[0:256, :]) + _dot(nh_ref[0, r, :], wo_ref[256:512, :])
               + _dot(na_ref[0, r, :], wo_ref[512:1024, :]))
        x1 = x_ref[0, r, :] + gta_ref[0] * (_rms(mix) * gpm_ref[...])
        h = ((_rms(x1) * gpre_ref[...]) * (1.0 + scm_ref[0]) + shm_ref[0]).astype(BF16)
        return x1, h

    def mlp(i, x1, h):
        acc = None
        for j in range(dff // ff_chunk):
            sl = slice(j * ff_chunk, (j + 1) * ff_chunk)
            hj = jnp.square(jnp.maximum(_dot(h, w1_ref[:, sl]), 0.0)).astype(BF16)
            part = _dot(hj, w2_ref[sl, :])
            acc = part if acc is None else acc + part
        o_ref[0, i * sub:(i + 1) * sub, :] = x1 + gtm_ref[0] * (_rms(acc) * gpost_ref[...])

    nxt = mixer_out(0)
    for i in range(tm // sub):
        cur = nxt
        if i + 1 < tm // sub:
            nxt = mixer_out(i + 1)
        mlp(i, *cur)


def _out_mlp(x3, nr, nh, na, mods, mod_row, g_post_mix, g_pre_mlp, g_post_mlp, wo16, w116, w216, l, tm):
    G, R, D = x3.shape
    row = lambda w: pl.BlockSpec((1, tm, w), lambda g, i: (g, i, 0))
    mod = lambda j: pl.BlockSpec((1, 1, D), lambda g, i: (mod_row(g), 0, j))
    return pl.pallas_call(
        functools.partial(_out_mlp_kernel, ff_chunk=1024, sub=min(512, tm)),
        grid=(G, R // tm),
        in_specs=[row(D), row(RET_V), row(HY_CH), row(ATT_Q), mod(2), mod(3), mod(4), mod(5),
                  _const_spec((1, D)), _const_spec((1, D)), _const_spec((1, D)),
                  _layer_spec(wo16, l), _layer_spec(w116, l), _layer_spec(w216, l)],
        out_specs=row(D),
        out_shape=jax.ShapeDtypeStruct((G, R, D), F32),
        compiler_params=_cparams(("parallel", "parallel")),
        name="out_mlp",
    )(x3, nr, nh, na, mods, mods, mods, mods, g_post_mix, g_pre_mlp, g_post_mlp, wo16, w116, w216)


def _rot_tables(cos, sin):
    lane = jnp.arange(LANES) % 32
    first = (lane < 16)[None, :]
    return cos, jnp.where(first, -sin, 0.0), jnp.where(first, 0.0, sin)


def _rotary_tables(L):
    pos = jnp.arange(L)
    row = (pos // GRID_W).astype(F32)
    col = (pos % GRID_W).astype(F32)
    half = ATT_HD // 2
    inv_ax = ROPE_BASE ** (-jnp.arange(0, half, 2, dtype=F32) / half)

    def tab(p, inv):
        ang = p[:, None] * inv[None, :]
        ang = jnp.concatenate([ang, ang], axis=-1)
        return jnp.cos(ang), jnp.sin(ang)

    cr, sr = tab(row, inv_ax)
    cc, sc = tab(col, inv_ax)
    inv_ret = 1.0 / (RET_ROT_BASE ** jnp.linspace(0.0, 1.0, RET_DK // 2, dtype=F32))
    ct, st = tab(pos.astype(F32), inv_ret)
    ret = _rot_tables(jnp.tile(ct, (1, 4)), jnp.tile(st, (1, 4)))
    ax = _rot_tables(jnp.tile(jnp.concatenate([cr, cc], -1), (1, 2)),
                     jnp.tile(jnp.concatenate([sr, sc], -1), (1, 2)))
    return ret + ax


def _dft_mats(L):
    n = 2 * L
    k = jnp.arange(L // 2, dtype=jnp.int32)

    def pair(t):
        ang = ((k[:, None] * t[None, :]) % n).astype(F32) * (2.0 * math.pi / n)
        return jnp.cos(ang).astype(BF16), (-jnp.sin(ang)).astype(BF16)

    ce, se = pair(2 * k)
    co, so = pair(2 * k + 1)
    return ce, se, co, so, co.T, so.T


def _filter_features(L):
    t = jnp.linspace(0.0, 1.0, L, dtype=F32)[:, None]
    w = 2.0 * math.pi * jnp.arange(L, dtype=F32) / L
    bands = jnp.linspace(1e-4, HY_BANDS - 1, HY_BANDS, dtype=F32)
    ang = w[:, None] * bands[None, :]
    z = jnp.concatenate([t, jnp.cos(ang), -jnp.sin(ang)], axis=-1)
    zf = jnp.pad(z, ((0, 0), (0, LANES - HY_EMB)))
    max_decay = math.log(HY_TARGET) / HY_FAST_DECAY
    min_decay = math.log(HY_TARGET) / HY_SLOW_DECAY
    deltas = jnp.linspace(min_decay, max_decay, HY_CH, dtype=F32)
    dec = jnp.exp(-t * jnp.abs(deltas)[None, :])
    return zf, dec


def _row_tile(rows):
    return 1024 if rows % 1024 == 0 else 512


def _pad_to(a, shape):
    return jnp.pad(a, [(0, s - d) for d, s in zip(a.shape, shape)])


def kernel(x, c, ctx, c_ctx, w_ada, b_ada, g_pre_mix, g_post_mix, g_pre_mlp, g_post_mlp, w_in,
           ret_decay_fwd, ret_decay_bwd, hy_short_w, hy_short_b, hy_f_w1, hy_f_b1, hy_f_w2, hy_f_b2,
           hy_f_w3, hy_f_freq, hy_bias, attn_sink, g_ret, g_hy, g_att, w_out, w_ff1, w_ff2):
    B, L, D = x.shape
    Lc = ctx.shape[1]
    depth = w_ada.shape[0]
    assert D == 1024 and w_in.shape[2] == 2304 and L % 256 == 0 and Lc % 256 == 0 and L >= 3 * ATT_BLOCK

    rows = -(-(B + 1) // 8) * 8
    cc = _pad_to(jnp.concatenate([c, c_ctx[None, :]], axis=0), (rows, D))
    mods_all = _ada(cc, w_ada, b_ada).reshape(depth, rows, 1, 6 * D)

    w_in16 = w_in.astype(BF16)
    w_out16 = w_out.astype(BF16)
    w_ff116 = w_ff1.astype(BF16)
    w_ff216 = w_ff2.astype(BF16)

    tabs = _rotary_tables(L)
    mats = {n: _dft_mats(n) for n in {L, Lc}}
    feats = {n: _filter_features(n) for n in {L, Lc}}

    ctx_rows = B * Lc
    tm = _row_tile(L)
    tc = _row_tile(ctx_rows)
    xc = ctx.reshape(ctx_rows // tc, tc, D)
    lat_row = lambda g: g
    ctx_row = lambda g: B

    for l in range(depth):
        last = l == depth - 1
        mods = mods_all[l]
        dec2 = jnp.stack([ret_decay_fwd[l], ret_decay_bwd[l]])
        dq = jnp.repeat(dec2, RET_DK, axis=1)
        ds = jnp.repeat(dec2, RET_CHUNK, axis=1)
        dv = jnp.repeat(dec2, RET_DV, axis=1)
        gr, gh, ga = g_ret[l][None], g_hy[l][None], g_att[l][None]
        gpre, gpm = g_pre_mix[l][None], g_post_mix[l][None]
        gprm, gpom = g_pre_mlp[l][None], g_post_mlp[l][None]
        fw1 = _pad_to(hy_f_w1[l], (LANES, LANES))
        fb1 = _pad_to(hy_f_b1[l][None], (1, LANES))
        fw2 = _pad_to(hy_f_w2[l], (2, LANES, LANES))
        fb2 = _pad_to(hy_f_b2[l][:, None, :], (2, 1, LANES))
        fw3 = _pad_to(hy_f_w3[l], (LANES, 2 * HY_CH))
        ffr = _pad_to(hy_f_freq[l][None], (1, LANES))
        sw, sb, hbias = hy_short_w[l], hy_short_b[l][None], hy_bias[l][None]

        def spectrum(n):
            zf, dec = feats[n]
            return _hyena_spectrum(n, zf, fw1, fb1, fw2, fb2, fw3, ffr, dec, mats[n])

        pc16, pc32 = _in_proj(xc, mods, ctx_row, gpre, w_in16, l, None, tc)
        pc16 = pc16.reshape(B, Lc, P16_W)
        pc32 = pc32.reshape(B, Lc, P32_W)
        cret, cstate = _retention(pc16, pc32, dq, ds, dv, gr, None, want_out=not last)

        p16, p32 = _in_proj(x, mods, lat_row, gpre, w_in16, l, tabs, tm)
        ret, _ = _retention(p16, p32, dq, ds, dv, gr, cstate)
        hyo = _hyena(p32, sw, sb, hbias, gh, mats[L], spectrum(L))
        att = _attention(attn_sink[l], p16, pc16, ga, True)
        x = _out_mlp(x, ret, hyo, att, mods, lat_row, gpm, gprm, gpom,
                     w_out16, w_ff116, w_ff216, l, tm)

        if not last:
            chyo = _hyena(pc32, sw, sb, hbias, gh, mats[Lc], spectrum(Lc))
            catt = _attention(attn_sink[l], None, pc16, ga, False)
            r3 = lambda a: a.reshape(ctx_rows // tc, tc, a.shape[-1])
            xc = _out_mlp(xc, r3(cret), r3(chyo), r3(catt), mods, ctx_row, gpm, gprm, gpom,
                          w_out16, w_ff116, w_ff216, l, tc)
    return x
```

```python
import functools
import math

import jax
import jax.numpy as jnp
from jax import lax
from jax.experimental import pallas as pl
from jax.experimental.pallas import tpu as pltpu

F32 = jnp.float32
BF16 = jnp.bfloat16
EPS = 1e-6
NEG_INF = -1e30
LOG2E = math.log2(math.e)

RET_HEADS = 4
RET_DK = 32
RET_DV = 64
RET_QK = RET_HEADS * RET_DK
RET_V = RET_HEADS * RET_DV
RET_CHUNK = 128
RET_UNROLL = 4
RET_ROT_BASE = 10000.0
HY_CH = 256
HY_EMB = 33
HY_BANDS = 16
HY_ORDER = 64
HY_FAST_DECAY = 0.3
HY_SLOW_DECAY = 1.5
HY_TARGET = 1e-2
ATT_HEADS = 8
ATT_KV_HEADS = 2
ATT_HD = 64
ATT_Q = ATT_HEADS * ATT_HD
ATT_KV = ATT_KV_HEADS * ATT_HD
WINDOW = 128
ATT_BLOCK = 128
ATT_UNROLL = 8
ROPE_BASE = 10000.0
GRID_W = 64

P16_W = 2 * RET_QK + RET_V + ATT_Q + 2 * ATT_KV
P32_W = RET_V + 3 * HY_CH

LANES = 128
VMEM_LIMIT = 56 * 1024 * 1024


def _cparams(sem):
    return pltpu.CompilerParams(dimension_semantics=sem, vmem_limit_bytes=VMEM_LIMIT)


def _const_spec(shape):
    nd = len(shape)
    return pl.BlockSpec(shape, lambda *_: (0,) * nd, pipeline_mode=pl.Buffered(1))


def _layer_spec(stacked, l):
    nd = stacked.ndim - 1
    return pl.BlockSpec((None,) + stacked.shape[1:], lambda *_: (l,) + (0,) * nd,
                        pipeline_mode=pl.Buffered(1))


def _rms(x):
    return x * lax.rsqrt(jnp.mean(x * x, axis=-1, keepdims=True) + EPS)


def _dot(a, b):
    return jnp.dot(a, b, preferred_element_type=F32)


def _dot_nt(a, b):
    return lax.dot_general(a, b, (((1,), (1,)), ((), ())), preferred_element_type=F32)


def _dot_tn(a, b):
    return lax.dot_general(a, b, (((0,), (0,)), ((), ())), preferred_element_type=F32)


def _split(a):
    hi = a.astype(BF16)
    lo = (a - hi.astype(F32)).astype(BF16)
    return hi, lo


def _dot3(a, b):
    ah, al = _split(a)
    bh, bl = _split(b)
    return _dot(ah, bh) + _dot(al, bh) + _dot(ah, bl)


def _ada_kernel(c_ref, w_ref, b_ref, o_ref):
    cv = c_ref[...]
    s = cv * jax.nn.sigmoid(cv)
    o_ref[0] = _dot(s.astype(BF16), w_ref[0].astype(BF16)) + b_ref[0]


def _ada(cc, w_ada, b_ada):
    depth, d, d6 = w_ada.shape
    rows = cc.shape[0]
    tn = 1536
    return pl.pallas_call(
        _ada_kernel,
        grid=(depth, d6 // tn),
        in_specs=[
            pl.BlockSpec((rows, d), lambda l, j: (0, 0)),
            pl.BlockSpec((1, d, tn), lambda l, j: (l, 0, j)),
            pl.BlockSpec((1, 1, tn), lambda l, j: (l, 0, j)),
        ],
        out_specs=pl.BlockSpec((1, rows, tn), lambda l, j: (l, 0, j)),
        out_shape=jax.ShapeDtypeStruct((depth, rows, d6), F32),
        compiler_params=_cparams(("arbitrary", "arbitrary")),
        name="ada",
    )(cc, w_ada, b_ada.reshape(depth, 1, d6))


def _rot(x, cos, sa, sb):
    return x * cos + pltpu.roll(x, LANES - 16, 1) * sa + pltpu.roll(x, 16, 1) * sb


def _in_kernel(x_ref, sh_ref, sc_ref, g_ref, w_ref, *rest, rotary, sub):
    if rotary:
        rc, rsa, rsb, ac, asa, asb, o16_ref, o32_ref = rest
    else:
        o16_ref, o32_ref = rest
    tm = x_ref.shape[1]
    k_scale = RET_DK ** -0.5
    q_scale = ATT_HD ** -0.5 * LOG2E

    def normed(i):
        x = x_ref[0, i * sub:(i + 1) * sub, :]
        return ((_rms(x) * g_ref[...]) * (1.0 + sc_ref[0]) + sh_ref[0]).astype(BF16)

    def project(i, hb):
        r = slice(i * sub, (i + 1) * sub)

        def rot(v, tabs):
            return _rot(v, *[t[r, :] for t in tabs]) if rotary else v

        ret_tabs = (rc, rsa, rsb) if rotary else None
        ax_tabs = (ac, asa, asb) if rotary else None
        pa = _dot(hb, w_ref[:, 1536:2304])
        for j in range(4):
            aq = rot(pa[:, 128 * j:128 * (j + 1)], ax_tabs)
            o16_ref[0, r, 512 + 128 * j:640 + 128 * j] = (aq * q_scale).astype(BF16)
        o16_ref[0, r, 1024:1152] = rot(pa[:, 512:640], ax_tabs).astype(BF16)
        o16_ref[0, r, 1152:1280] = pa[:, 640:768].astype(BF16)
        pr = _dot(hb, w_ref[:, 0:512])
        o16_ref[0, r, 0:128] = rot(pr[:, 0:128], ret_tabs).astype(BF16)
        o16_ref[0, r, 128:256] = (rot(pr[:, 128:256], ret_tabs) * k_scale).astype(BF16)
        o16_ref[0, r, 256:512] = pr[:, 256:512].astype(BF16)
        o32_ref[0, r, :] = _dot(hb, w_ref[:, 512:1536])

    nxt = normed(0)
    for i in range(tm // sub):
        cur = nxt
        if i + 1 < tm // sub:
            nxt = normed(i + 1)
        project(i, cur)


def _in_proj(x3, mods, mod_row, g_pre, w_in16, l, tabs, tm):
    G, R, D = x3.shape
    rotary = tabs is not None
    nt = R // tm
    in_specs = [
        pl.BlockSpec((1, tm, D), lambda g, i: (g, i, 0)),
        pl.BlockSpec((1, 1, D), lambda g, i: (mod_row(g), 0, 0)),
        pl.BlockSpec((1, 1, D), lambda g, i: (mod_row(g), 0, 1)),
        _const_spec((1, D)),
        _layer_spec(w_in16, l),
    ]
    args = [x3, mods, mods, g_pre, w_in16]
    if rotary:
        in_specs += [pl.BlockSpec((tm, LANES), lambda g, i: (i, 0))] * 6
        args += list(tabs)
    return pl.pallas_call(
        functools.partial(_in_kernel, rotary=rotary, sub=min(256, tm)),
        grid=(G, nt),
        in_specs=in_specs,
        out_specs=[
            pl.BlockSpec((1, tm, P16_W), lambda g, i: (g, i, 0)),
            pl.BlockSpec((1, tm, P32_W), lambda g, i: (g, i, 0)),
        ],
        out_shape=[
            jax.ShapeDtypeStruct((G, R, P16_W), BF16),
            jax.ShapeDtypeStruct((G, R, P32_W), F32),
        ],
        compiler_params=_cparams(("parallel", "parallel")),
        name="in_proj_rot" if rotary else "in_proj",
    )(*args)


def _ret_kernel(qkv_ref, rg_ref, dq_ref, ds_ref, dv_ref, g_ref, *rest, L, has_init, want_out):
    rest = list(rest)
    s0_ref = rest.pop(0) if has_init else None
    o_ref = rest.pop(0) if want_out else None
    sfin_ref, st_scr, dec_scr, dmask_scr, cdec_scr = rest
    C = RET_CHUNK
    N = L // C

    @pl.when(pl.program_id(0) == 0)
    def _():
        def log_gamma(ref, r):
            return jnp.log1p(-jnp.exp(ref[r:r + 1, :]))

        lfq, lbq = log_gamma(dq_ref, 0), log_gamma(dq_ref, 1)
        lfs, lbs = log_gamma(ds_ref, 0), log_gamma(ds_ref, 1)
        ri = lax.broadcasted_iota(jnp.int32, (C, LANES), 0).astype(F32)
        dec_scr[0] = jnp.exp(lfq * (ri + 1.0))
        dec_scr[1] = jnp.exp(lfq * (C - 1.0 - ri))
        dec_scr[2] = jnp.exp(lbq * (C - ri))
        dec_scr[3] = jnp.exp(lbq * ri)
        di = lax.broadcasted_iota(jnp.int32, (C, 4 * C), 0)
        dj = lax.broadcasted_iota(jnp.int32, (C, 4 * C), 1) & (C - 1)
        diff = (di - dj).astype(F32)
        dmask_scr[...] = (jnp.where(diff >= 0, jnp.exp(lfs * jnp.maximum(diff, 0.0)), 0.0)
                          + jnp.where(diff <= 0, jnp.exp(lbs * jnp.maximum(-diff, 0.0)), 0.0))
        cdec_scr[0:1, :] = jnp.exp(log_gamma(dv_ref, 0) * float(C))
        cdec_scr[1:2, :] = jnp.exp(log_gamma(dv_ref, 1) * float(C))

    cdec_f, cdec_b = cdec_scr[0:1, :], cdec_scr[1:2, :]

    lane_q = lax.broadcasted_iota(jnp.int32, (1, RET_QK), 1) >> 5
    lane_v = lax.broadcasted_iota(jnp.int32, (1, RET_V), 1) >> 6
    hm = [jnp.where(lane_q == h, 1.0, 0.0).astype(BF16) for h in range(RET_HEADS)]
    cm = [jnp.where(lane_v == h, 1.0, 0.0).astype(BF16) for h in range(RET_HEADS)]
    bd = ((lax.broadcasted_iota(jnp.int32, (RET_QK, RET_V), 0) >> 5)
          == (lax.broadcasted_iota(jnp.int32, (RET_QK, RET_V), 1) >> 6))
    ones64 = jnp.where((lax.broadcasted_iota(jnp.int32, (RET_V, RET_V), 0) >> 6)
                       == (lax.broadcasted_iota(jnp.int32, (RET_V, RET_V), 1) >> 6), 1.0, 0.0).astype(BF16)

    if has_init:
        sf0 = s0_ref[0, 0]
        sb0 = s0_ref[0, 1]
    else:
        sf0 = jnp.zeros((RET_QK, RET_V), F32)
        sb0 = jnp.zeros((RET_QK, RET_V), F32)

    def kv_update(k16, v16, kdec, cdec, s):
        kd = (k16.astype(F32) * kdec).astype(BF16)
        return cdec * s + jnp.where(bd, _dot_tn(kd, v16), 0.0)

    def kv_at(n):
        r0 = pl.multiple_of(n * C, C)
        return qkv_ref[0, pl.ds(r0, C), 128:256], qkv_ref[0, pl.ds(r0, C), 256:512]

    U = min(RET_UNROLL, N)

    def scan_body(it, carry):
        sf, sb = carry
        for u in range(U):
            nf = it * U + u
            nb = N - 1 - nf
            st_scr[nf, 0:RET_QK, :] = sf.astype(BF16)
            sf = kv_update(*kv_at(nf), dec_scr[1], cdec_f, sf)
            st_scr[nb, RET_QK:2 * RET_QK, :] = sb.astype(BF16)
            sb = kv_update(*kv_at(nb), dec_scr[3], cdec_b, sb)
        return sf, sb

    sf_fin, sb_fin = lax.fori_loop(0, N // U, scan_body, (sf0, sb0))
    sfin_ref[0, 0] = sf_fin
    sfin_ref[0, 1] = sb_fin

    if not want_out:
        return

    def out_body(it, carry):
        ns = [it * U + u for u in range(U)]
        rows = [pl.ds(pl.multiple_of(n * C, C), C) for n in ns]
        q16s = [qkv_ref[0, r, 0:128] for r in rows]
        kvs = [kv_at(n) for n in ns]
        scores = [_dot_nt(q16, jnp.concatenate([k16 * hm[h] for h in range(RET_HEADS)], axis=0))
                  for q16, (k16, _) in zip(q16s, kvs)]
        outs = []
        for n, q16, (_, v16), s in zip(ns, q16s, kvs, scores):
            qf = q16.astype(F32)
            vblk = jnp.concatenate([v16 * cm[h] for h in range(RET_HEADS)], axis=0)
            lhs = jnp.concatenate([(s * dmask_scr[...]).astype(BF16), (qf * dec_scr[0]).astype(BF16),
                                   (qf * dec_scr[2]).astype(BF16)], axis=1)
            outs.append(_dot(lhs, jnp.concatenate([vblk, st_scr[n]], axis=0)))
        mss = [_dot((o * o).astype(BF16), ones64) * (1.0 / RET_DV) for o in outs]
        for r, o, ms in zip(rows, outs, mss):
            rg = rg_ref[0, r, :]
            gated = (o * lax.rsqrt(ms + EPS)) * (rg * jax.nn.sigmoid(rg))
            o_ref[0, r, :] = (_rms(gated) * g_ref[...]).astype(BF16)
        return carry

    lax.fori_loop(0, N // U, out_body, 0)


def _retention(p16, p32, dq, ds, dv, g_ret, s0, want_out=True):
    B, L, _ = p16.shape
    has_init = s0 is not None
    in_specs = [
        pl.BlockSpec((1, L, 512), lambda b: (b, 0, 0)),
        pl.BlockSpec((1, L, RET_V), lambda b: (b, 0, 0)),
        _const_spec(dq.shape), _const_spec(ds.shape), _const_spec(dv.shape),
        _const_spec((1, RET_V)),
    ]
    args = [p16, p32, dq, ds, dv, g_ret]
    if has_init:
        in_specs.append(pl.BlockSpec((1, 2, RET_QK, RET_V), lambda b: (b, 0, 0, 0)))
        args.append(s0)
    out_specs = [pl.BlockSpec((1, 2, RET_QK, RET_V), lambda b: (b, 0, 0, 0))]
    out_shape = [jax.ShapeDtypeStruct((B, 2, RET_QK, RET_V), F32)]
    if want_out:
        out_specs.insert(0, pl.BlockSpec((1, L, RET_V), lambda b: (b, 0, 0)))
        out_shape.insert(0, jax.ShapeDtypeStruct((B, L, RET_V), BF16))
    res = pl.pallas_call(
        functools.partial(_ret_kernel, L=L, has_init=has_init, want_out=want_out),
        grid=(B,),
        in_specs=in_specs,
        out_specs=out_specs,
        out_shape=out_shape,
        scratch_shapes=[pltpu.VMEM((L // RET_CHUNK, 2 * RET_QK, RET_V), BF16),
                        pltpu.VMEM((4, RET_CHUNK, LANES), F32),
                        pltpu.VMEM((RET_CHUNK, 4 * RET_CHUNK), F32),
                        pltpu.VMEM((2, RET_V), F32)],
        compiler_params=_cparams(("arbitrary",)),
        name="retention_init" if has_init else "retention",
    )(*args)
    return tuple(res) if want_out else (None, res[0])


def _halves(scr, start, rows):
    return jnp.concatenate([scr[0, pl.ds(start, rows, stride=2), :],
                            scr[1, pl.ds(start, rows, stride=2), :]], axis=1)


def _alt_sign(rows, first=0):
    r = lax.broadcasted_iota(jnp.int32, (rows, HY_CH), 0) + first
    return jnp.where((r & 1) == 1, -1.0, 1.0)


def _filt_kernel(zf_ref, w1_ref, b1_ref, w2_ref, b2_ref, w3_ref, fr_ref, dec_ref,
                 ce_ref, se_ref, co_ref, so_ref, hra_ref, hia_ref, hrb_ref, hib_ref, hm_ref,
                 a_scr, d_scr, *, L):
    n = 2 * L
    M = L // 2
    fr = fr_ref[...]
    h = jnp.sin(fr * (_dot3(zf_ref[...], w1_ref[...]) + b1_ref[...]))
    for j in range(2):
        h = jnp.sin(fr * (_dot3(h, w2_ref[j]) + b2_ref[j]))
    h = _dot3(h, w3_ref[...])
    dec = dec_ref[...]
    row = lax.broadcasted_iota(jnp.int32, (L, HY_CH), 0)
    hf = h[:, 0:HY_CH] * dec
    hb = jnp.where(row == 0, 0.0, h[:, HY_CH:2 * HY_CH] * dec)
    for scr, val in ((a_scr, hf + hb), (d_scr, hf - hb)):
        scr[0] = val[:, 0:LANES]
        scr[1] = val[:, LANES:2 * LANES]
    ae, ao = _halves(a_scr, 0, M), _halves(a_scr, 1, M)
    de, do = _halves(d_scr, 0, M), _halves(d_scr, 1, M)
    sgn = _alt_sign(M)
    hm_ref[0:1, :] = jnp.sum(ae * sgn, axis=0, keepdims=True) * (2.0 / n)
    hm_ref[1:2, :] = jnp.sum(do * sgn, axis=0, keepdims=True) * (-2.0 / n)

    def dft(m_ref, x):
        xh, xl = _split(x)
        return _dot(m_ref[...], xh) + _dot(m_ref[...], xl)

    wk = jnp.where(lax.broadcasted_iota(jnp.int32, (M, 1), 0) == 0, 1.0 / n, 2.0 / n)
    ea, ta = dft(ce_ref, ae), dft(co_ref, ao)
    hra_ref[...] = (ea + ta) * wk
    hrb_ref[...] = (ea - ta) * wk
    ed, td = dft(se_ref, de), dft(so_ref, do)
    hia_ref[...] = (ed + td) * wk
    hib_ref[...] = (td - ed) * wk


def _hyena_spectrum(L, zf, w1, b1, w2, b2, w3, fr, dec, mats):
    M = L // 2
    args = (zf, w1, b1, w2, b2, w3, fr, dec) + tuple(mats[:4])
    half = jax.ShapeDtypeStruct((M, HY_CH), F32)
    return pl.pallas_call(
        functools.partial(_filt_kernel, L=L),
        grid=(1,),
        in_specs=[_const_spec(a.shape) for a in args],
        out_specs=[_const_spec((M, HY_CH))] * 4 + [_const_spec((2, HY_CH))],
        out_shape=[half] * 4 + [jax.ShapeDtypeStruct((2, HY_CH), F32)],
        scratch_shapes=[pltpu.VMEM((2, L, LANES), F32)] * 2,
        compiler_params=_cparams(("arbitrary",)),
        name="hyena_filter",
    )(*args)


def _hy_kernel(v_ref, x1_ref, x0_ref, sw_ref, sb_ref, bias_ref, g_ref,
               ce_ref, se_ref, co_ref, so_ref, cot_ref, sot_ref,
               hra_ref, hia_ref, hrb_ref, hib_ref, hm_ref, o_ref,
               z_scr, x0_scr, out_scr, ger_scr, gei_scr, gor_scr, goi_scr, *, L, R):
    nchunks = L // R
    M = L // 2
    RB = min(256, M)
    for c in range(nchunks):
        r0 = c * R
        lo = max(r0 - 8, 0)
        hi = min(r0 + R + 8, L)
        rows = hi - lo
        off = r0 - lo
        grow = lax.broadcasted_iota(jnp.int32, (R, HY_CH), 0) + r0

        def conv(ref, c0):
            ext = ref[0, lo:hi, :]
            up = pltpu.roll(ext, 1, 0)[off:off + R]
            un = pltpu.roll(ext, rows - 1, 0)[off:off + R]
            if c == 0:
                up = jnp.where(grow == 0, 0.0, up)
            if c == nchunks - 1:
                un = jnp.where(grow == L - 1, 0.0, un)
            u = ref[0, r0:r0 + R, :]
            return (up * sw_ref[0:1, c0:c0 + HY_CH] + u * sw_ref[1:2, c0:c0 + HY_CH]
                    + un * sw_ref[2:3, c0:c0 + HY_CH] + sb_ref[:, c0:c0 + HY_CH])

        z = conv(v_ref, 0) * conv(x1_ref, HY_CH)
        x0 = conv(x0_ref, 2 * HY_CH)
        for hlf in range(2):
            z_scr[hlf, r0:r0 + R, :] = z[:, hlf * LANES:(hlf + 1) * LANES]
            x0_scr[hlf, r0:r0 + R, :] = x0[:, hlf * LANES:(hlf + 1) * LANES]

    ze, zo = _halves(z_scr, 0, M), _halves(z_scr, 1, M)
    sgn = _alt_sign(M)
    mid_r = jnp.sum(ze * sgn, axis=0, keepdims=True)
    mid_i = -jnp.sum(zo * sgn, axis=0, keepdims=True)
    ym_r = mid_r * hm_ref[0:1, :] - mid_i * hm_ref[1:2, :]
    ym_i = mid_r * hm_ref[1:2, :] + mid_i * hm_ref[0:1, :]
    zeb, zob = ze.astype(BF16), zo.astype(BF16)
    for kb in range(M // RB):
        ks = slice(kb * RB, (kb + 1) * RB)
        er, ei = _dot(ce_ref[ks, :], zeb), _dot(se_ref[ks, :], zeb)
        tr, ti = _dot(co_ref[ks, :], zob), _dot(so_ref[ks, :], zob)
        ar, ai, br, bi = er + tr, ei + ti, er - tr, ti - ei
        hra, hia, hrb, hib = hra_ref[ks, :], hia_ref[ks, :], hrb_ref[ks, :], hib_ref[ks, :]
        yar, yai = ar * hra - ai * hia, ar * hia + ai * hra
        ybr, ybi = br * hrb - bi * hib, br * hib + bi * hrb
        ger_scr[ks, :] = (yar + ybr).astype(BF16)
        gei_scr[ks, :] = (yai - ybi).astype(BF16)
        gor_scr[ks, :] = (yar - ybr).astype(BF16)
        goi_scr[ks, :] = (yai + ybi).astype(BF16)

    ger, gei, gor, goi = ger_scr[...], gei_scr[...], gor_scr[...], goi_scr[...]
    for tb in range(M // RB):
        ts = slice(tb * RB, (tb + 1) * RB)
        sg = _alt_sign(RB, tb * RB)
        ys = (_dot(ce_ref[ts, :], ger) + _dot(se_ref[ts, :], gei) + sg * ym_r,
              _dot(cot_ref[ts, :], gor) + _dot(sot_ref[ts, :], goi) - sg * ym_i)
        for par, y in enumerate(ys):
            first = 2 * tb * RB + par
            out = (y + _halves(z_scr, first, RB) * bias_ref[...]) * _halves(x0_scr, first, RB)
            out = _rms(out) * g_ref[...]
            for hlf in range(2):
                out_scr[hlf, pl.ds(first, RB, stride=2), :] = out[:, hlf * LANES:(hlf + 1) * LANES]

    for c in range(nchunks):
        r = slice(c * R, (c + 1) * R)
        o_ref[0, r, :] = jnp.concatenate([out_scr[0, r, :], out_scr[1, r, :]], axis=1).astype(BF16)


def _hyena(p32, short_w, short_b, hy_bias, g_hy, mats, spec):
    B, L, _ = p32.shape
    R = min(256, L)
    M = L // 2
    col = lambda j: pl.BlockSpec((1, L, HY_CH), lambda b: (b, 0, j))
    consts = (short_w, short_b, hy_bias, g_hy) + tuple(mats) + tuple(spec)
    return pl.pallas_call(
        functools.partial(_hy_kernel, L=L, R=R),
        grid=(B,),
        in_specs=[col(1), col(2), col(3)] + [_const_spec(a.shape) for a in consts],
        out_specs=pl.BlockSpec((1, L, HY_CH), lambda b: (b, 0, 0)),
        out_shape=jax.ShapeDtypeStruct((B, L, HY_CH), BF16),
        scratch_shapes=[pltpu.VMEM((2, L, LANES), F32)] * 3 + [pltpu.VMEM((M, HY_CH), BF16)] * 4,
        compiler_params=_cparams(("parallel",)),
        name="hyena",
    )(p32, p32, p32, *consts)


def _expand_kv(kv_ref, k2_scr, vv_scr, rows):
    R = min(256, rows)
    lo = lax.broadcasted_iota(jnp.int32, (R, LANES), 1) < ATT_HD
    ones = jnp.ones((R, LANES), BF16)
    for c in range(rows // R):
        sl = slice(c * R, (c + 1) * R)
        k = kv_ref[0, sl, 0:128].astype(F32)
        v = kv_ref[0, sl, 128:256].astype(F32)
        kr = pltpu.roll(k, ATT_HD, 1)
        vr = pltpu.roll(v, ATT_HD, 1)
        k2_scr[0, sl, :] = jnp.where(lo, k, kr).astype(BF16)
        k2_scr[1, sl, :] = jnp.where(lo, kr, k).astype(BF16)
        vv_scr[0, sl, 0:128] = jnp.where(lo, v, vr).astype(BF16)
        vv_scr[1, sl, 0:128] = jnp.where(lo, vr, v).astype(BF16)
        vv_scr[0, sl, 128:256] = ones
        vv_scr[1, sl, 128:256] = ones


def _att_kernel(sink_ref, q_ref, *rest, L, Lc, band):
    if band:
        (kv_ref, ckv_ref, g_ref, o_ref, k2_scr, vv_scr, ck2_scr, cvv_scr, att_scr) = rest
        _expand_kv(kv_ref, k2_scr, vv_scr, L)
    else:
        (ckv_ref, g_ref, o_ref, ck2_scr, cvv_scr, att_scr) = rest
    _expand_kv(ckv_ref, ck2_scr, cvv_scr, Lc)

    T = ATT_BLOCK
    nb = L // T
    lane = lax.broadcasted_iota(jnp.int32, (1, LANES), 1)
    lo = lane < ATT_HD
    lo16 = jnp.where(lo, 1.0, 0.0).astype(BF16)
    hi16 = jnp.where(lo, 0.0, 1.0).astype(BF16)
    hrow = lax.broadcasted_iota(jnp.int32, (4 * T, 1), 0) >> 7
    ti = lax.broadcasted_iota(jnp.int32, (T, T), 0)
    tj = lax.broadcasted_iota(jnp.int32, (T, T), 1)
    tri_prev = jnp.where(tj >= ti, 0.0, NEG_INF)
    tri_next = jnp.where(tj <= ti, 0.0, NEG_INF)

    def add_bias(s, bias):
        return (s.reshape(4, T, T) + bias[None]).reshape(4 * T, T)

    U = min(ATT_UNROLL, nb)

    def step(it, carry):
        blocks = []
        for u in range(U):
            n = it * U + u
            blk = dict(r0=pl.multiple_of(n * T, T))
            if band:
                blk.update(rp=pl.multiple_of(jnp.maximum(n - 1, 0) * T, T),
                           rn=pl.multiple_of(jnp.minimum(n + 1, nb - 1) * T, T),
                           bias_p=tri_prev + jnp.where(n == 0, NEG_INF, 0.0),
                           bias_n=tri_next + jnp.where(n == nb - 1, NEG_INF, 0.0))
            blocks.append(blk)
        items = [(u, g) for u in range(U) for g in range(ATT_KV_HEADS)]

        def keys_of(scr, cscr, blk, g):
            parts = [cscr[g]]
            if band:
                parts += [scr[g, pl.ds(blk[k], T), :] for k in ("rp", "r0", "rn")]
            return jnp.concatenate(parts, axis=0)

        scores, sinks = [], []
        for u, g in items:
            qb = q_ref[0, pl.ds(blocks[u]["r0"], T), 256 * g:256 * (g + 1)]
            qs = []
            for p in range(2):
                qp = qb[:, 128 * p:128 * (p + 1)]
                qs += [qp * lo16, qp * hi16]
            qst = jnp.concatenate(qs, axis=0)
            scores.append(_dot_nt(qst, keys_of(k2_scr if band else None, ck2_scr, blocks[u], g)))
            sinks.append(LOG2E * jnp.where(
                hrow == 0, sink_ref[4 * g],
                jnp.where(hrow == 1, sink_ref[4 * g + 1],
                          jnp.where(hrow == 2, sink_ref[4 * g + 2], sink_ref[4 * g + 3]))))
        probs, esks = [], []
        for (u, g), s, sk in zip(items, scores, sinks):
            cols = [s[:, j * T:(j + 1) * T] for j in range(s.shape[1] // T)]
            if band:
                cols[-3] = add_bias(cols[-3], blocks[u]["bias_p"])
                cols[-1] = add_bias(cols[-1], blocks[u]["bias_n"])
            mx = cols[0]
            for c_ in cols[1:]:
                mx = jnp.maximum(mx, c_)
            m = jnp.maximum(jnp.max(mx, axis=-1, keepdims=True), sk)
            probs.append(jnp.concatenate([jnp.exp2(c_ - m).astype(BF16) for c_ in cols], axis=1))
            esks.append(jnp.exp2(sk - m))
        for (u, g), pr, esk in zip(items, probs, esks):
            pv = _dot(pr, keys_of(vv_scr if band else None, cvv_scr, blocks[u], g))
            res = pv[:, 0:LANES] / (pv[:, LANES:2 * LANES] + esk)
            for p in range(2):
                c0 = 128 * (2 * g + p)
                att_scr[u, :, c0:c0 + 128] = jnp.where(lo, res[2 * p * T:(2 * p + 1) * T],
                                                       res[(2 * p + 1) * T:(2 * p + 2) * T])
        for u in range(U):
            o_ref[0, pl.ds(blocks[u]["r0"], T), :] = (_rms(att_scr[u]) * g_ref[...]).astype(BF16)
        return carry

    lax.fori_loop(0, nb // U, step, 0)


def _attention(sink, p16, pc16, g_att, band):
    B, Lc, _ = pc16.shape
    L = p16.shape[1] if band else Lc
    qsrc = p16 if band else pc16
    kv_spec = lambda n: pl.BlockSpec((1, n, 256), lambda b: (b, 0, 4))
    in_specs = [pl.BlockSpec(memory_space=pltpu.SMEM),
                pl.BlockSpec((1, L, ATT_Q), lambda b: (b, 0, 1))]
    args = [sink, qsrc]
    scratch = []
    if band:
        in_specs.append(kv_spec(L))
        args.append(p16)
        scratch += [pltpu.VMEM((2, L, LANES), BF16), pltpu.VMEM((2, L, 2 * LANES), BF16)]
    in_specs += [kv_spec(Lc), _const_spec((1, ATT_Q))]
    args += [pc16, g_att]
    scratch += [pltpu.VMEM((2, Lc, LANES), BF16), pltpu.VMEM((2, Lc, 2 * LANES), BF16),
                pltpu.VMEM((ATT_UNROLL, ATT_BLOCK, ATT_Q), F32)]
    return pl.pallas_call(
        functools.partial(_att_kernel, L=L, Lc=Lc, band=band),
        grid=(B,),
        in_specs=in_specs,
        out_specs=pl.BlockSpec((1, L, ATT_Q), lambda b: (b, 0, 0)),
        out_shape=jax.ShapeDtypeStruct((B, L, ATT_Q), BF16),
        scratch_shapes=scratch,
        compiler_params=_cparams(("parallel",)),
        name="attn_window" if band else "attn_ctx",
    )(*args)


def _out_mlp_kernel(x_ref, nr_ref, nh_ref, na_ref, gta_ref, shm_ref, scm_ref, gtm_ref,
                    gpm_ref, gpre_ref, gpost_ref, wo_ref, w1_ref, w2_ref, o_ref, *, ff_chunk, sub):
    tm = x_ref.shape[1]
    dff = w1_ref.shape[1]

    def mixer_out(i):
        r = slice(i * sub, (i + 1) * sub)
        mix = (_dot(nr_ref[0, r, :], wo_ref[0:256, :]) + _dot(nh_ref[0, r, :], wo_ref[256:512, :])
               + _dot(na_ref[0, r, :], wo_ref[512:1024, :]))
        x1 = x_ref[0, r, :] + gta_ref[0] * (_rms(mix) * gpm_ref[...])
        h = ((_rms(x1) * gpre_ref[...]) * (1.0 + scm_ref[0]) + shm_ref[0]).astype(BF16)
        return x1, h

    def mlp(i, x1, h):
        acc = None
        for j in range(dff // ff_chunk):
            sl = slice(j * ff_chunk, (j + 1) * ff_chunk)
            hj = jnp.square(jnp.maximum(_dot(h, w1_ref[:, sl]), 0.0)).astype(BF16)
            part = _dot(hj, w2_ref[sl, :])
            acc = part if acc is None else acc + part
        o_ref[0, i * sub:(i + 1) * sub, :] = x1 + gtm_ref[0] * (_rms(acc) * gpost_ref[...])

    nxt = mixer_out(0)
    for i in range(tm // sub):
        cur = nxt
        if i + 1 < tm // sub:
            nxt = mixer_out(i + 1)
        mlp(i, *cur)


def _out_mlp(x3, nr, nh, na, mods, mod_row, g_post_mix, g_pre_mlp, g_post_mlp, wo16, w116, w216, l, tm):
    G, R, D = x3.shape
    row = lambda w: pl.BlockSpec((1, tm, w), lambda g, i: (g, i, 0))
    mod = lambda j: pl.BlockSpec((1, 1, D), lambda g, i: (mod_row(g), 0, j))
    return pl.pallas_call(
        functools.partial(_out_mlp_kernel, ff_chunk=1024, sub=min(512, tm)),
        grid=(G, R // tm),
        in_specs=[row(D), row(RET_V), row(HY_CH), row(ATT_Q), mod(2), mod(3), mod(4), mod(5),
                  _const_spec((1, D)), _const_spec((1, D)), _const_spec((1, D)),
                  _layer_spec(wo16, l), _layer_spec(w116, l), _layer_spec(w216, l)],
        out_specs=row(D),
        out_shape=jax.ShapeDtypeStruct((G, R, D), F32),
        compiler_params=_cparams(("parallel", "parallel")),
        name="out_mlp",
    )(x3, nr, nh, na, mods, mods, mods, mods, g_post_mix, g_pre_mlp, g_post_mlp, wo16, w116, w216)


def _rot_tables(cos, sin):
    lane = jnp.arange(LANES) % 32
    first = (lane < 16)[None, :]
    return cos, jnp.where(first, -sin, 0.0), jnp.where(first, 0.0, sin)


def _rotary_tables(L):
    pos = jnp.arange(L)
    row = (pos // GRID_W).astype(F32)
    col = (pos % GRID_W).astype(F32)
    half = ATT_HD // 2
    inv_ax = ROPE_BASE ** (-jnp.arange(0, half, 2, dtype=F32) / half)

    def tab(p, inv):
        ang = p[:, None] * inv[None, :]
        ang = jnp.concatenate([ang, ang], axis=-1)
        return jnp.cos(ang), jnp.sin(ang)

    cr, sr = tab(row, inv_ax)
    cc, sc = tab(col, inv_ax)
    inv_ret = 1.0 / (RET_ROT_BASE ** jnp.linspace(0.0, 1.0, RET_DK // 2, dtype=F32))
    ct, st = tab(pos.astype(F32), inv_ret)
    ret = _rot_tables(jnp.tile(ct, (1, 4)), jnp.tile(st, (1, 4)))
    ax = _rot_tables(jnp.tile(jnp.concatenate([cr, cc], -1), (1, 2)),
                     jnp.tile(jnp.concatenate([sr, sc], -1), (1, 2)))
    return ret + ax


def _dft_mats(L):
    n = 2 * L
    k = jnp.arange(L // 2, dtype=jnp.int32)

    def pair(t):
        ang = ((k[:, None] * t[None, :]) % n).astype(F32) * (2.0 * math.pi / n)
        return jnp.cos(ang).astype(BF16), (-jnp.sin(ang)).astype(BF16)

    ce, se = pair(2 * k)
    co, so = pair(2 * k + 1)
    return ce, se, co, so, co.T, so.T


def _filter_features(L):
    t = jnp.linspace(0.0, 1.0, L, dtype=F32)[:, None]
    w = 2.0 * math.pi * jnp.arange(L, dtype=F32) / L
    bands = jnp.linspace(1e-4, HY_BANDS - 1, HY_BANDS, dtype=F32)
    ang = w[:, None] * bands[None, :]
    z = jnp.concatenate([t, jnp.cos(ang), -jnp.sin(ang)], axis=-1)
    zf = jnp.pad(z, ((0, 0), (0, LANES - HY_EMB)))
    max_decay = math.log(HY_TARGET) / HY_FAST_DECAY
    min_decay = math.log(HY_TARGET) / HY_SLOW_DECAY
    deltas = jnp.linspace(min_decay, max_decay, HY_CH, dtype=F32)
    dec = jnp.exp(-t * jnp.abs(deltas)[None, :])
    return zf, dec


def _row_tile(rows):
    return 1024 if rows % 1024 == 0 else 512


def _pad_to(a, shape):
    return jnp.pad(a, [(0, s - d) for d, s in zip(a.shape, shape)])


def kernel(x, c, ctx, c_ctx, w_ada, b_ada, g_pre_mix, g_post_mix, g_pre_mlp, g_post_mlp, w_in,
           ret_decay_fwd, ret_decay_bwd, hy_short_w, hy_short_b, hy_f_w1, hy_f_b1, hy_f_w2, hy_f_b2,
           hy_f_w3, hy_f_freq, hy_bias, attn_sink, g_ret, g_hy, g_att, w_out, w_ff1, w_ff2):
    B, L, D = x.shape
    Lc = ctx.shape[1]
    depth = w_ada.shape[0]
    assert D == 1024 and w_in.shape[2] == 2304 and L % 256 == 0 and Lc % 256 == 0 and L >= 3 * ATT_BLOCK

    rows = -(-(B + 1) // 8) * 8
    cc = _pad_to(jnp.concatenate([c, c_ctx[None, :]], axis=0), (rows, D))
    mods_all = _ada(cc, w_ada, b_ada).reshape(depth, rows, 1, 6 * D)

    w_in16 = w_in.astype(BF16)
    w_out16 = w_out.astype(BF16)
    w_ff116 = w_ff1.astype(BF16)
    w_ff216 = w_ff2.astype(BF16)

    tabs = _rotary_tables(L)
    mats = {n: _dft_mats(n) for n in {L, Lc}}
    feats = {n: _filter_features(n) for n in {L, Lc}}

    ctx_rows = B * Lc
    tm = _row_tile(L)
    tc = _row_tile(ctx_rows)
    xc = ctx.reshape(ctx_rows // tc, tc, D)
    lat_row = lambda g: g
    ctx_row = lambda g: B

    for l in range(depth):
        last = l == depth - 1
        mods = mods_all[l]
        dec2 = jnp.stack([ret_decay_fwd[l], ret_decay_bwd[l]])
        dq = jnp.repeat(dec2, RET_DK, axis=1)
        ds = jnp.repeat(dec2, RET_CHUNK, axis=1)
        dv = jnp.repeat(dec2, RET_DV, axis=1)
        gr, gh, ga = g_ret[l][None], g_hy[l][None], g_att[l][None]
        gpre, gpm = g_pre_mix[l][None], g_post_mix[l][None]
        gprm, gpom = g_pre_mlp[l][None], g_post_mlp[l][None]
        fw1 = _pad_to(hy_f_w1[l], (LANES, LANES))
        fb1 = _pad_to(hy_f_b1[l][None], (1, LANES))
        fw2 = _pad_to(hy_f_w2[l], (2, LANES, LANES))
        fb2 = _pad_to(hy_f_b2[l][:, None, :], (2, 1, LANES))
        fw3 = _pad_to(hy_f_w3[l], (LANES, 2 * HY_CH))
        ffr = _pad_to(hy_f_freq[l][None], (1, LANES))
        sw, sb, hbias = hy_short_w[l], hy_short_b[l][None], hy_bias[l][None]

        def spectrum(n):
            zf, dec = feats[n]
            return _hyena_spectrum(n, zf, fw1, fb1, fw2, fb2, fw3, ffr, dec, mats[n])

        pc16, pc32 = _in_proj(xc, mods, ctx_row, gpre, w_in16, l, None, tc)
        pc16 = pc16.reshape(B, Lc, P16_W)
        pc32 = pc32.reshape(B, Lc, P32_W)
        cret, cstate = _retention(pc16, pc32, dq, ds, dv, gr, None, want_out=not last)

        p16, p32 = _in_proj(x, mods, lat_row, gpre, w_in16, l, tabs, tm)
        ret, _ = _retention(p16, p32, dq, ds, dv, gr, cstate)
        hyo = _hyena(p32, sw, sb, hbias, gh, mats[L], spectrum(L))
        att = _attention(attn_sink[l], p16, pc16, ga, True)
        x = _out_mlp(x, ret, hyo, att, mods, lat_row, gpm, gprm, gpom,
                     w_out16, w_ff116, w_ff216, l, tm)

        if not last:
            chyo = _hyena(pc32, sw, sb, hbias, gh, mats[Lc], spectrum(Lc))
            catt = _attention(attn_sink[l], None, pc16, ga, False)
            r3 = lambda a: a.reshape(ctx_rows // tc, tc, a.shape[-1])
            xc = _out_mlp(xc, r3(cret), r3(chyo), r3(catt), mods, ctx_row, gpm, gprm, gpom,
                          w_out16, w_ff116, w_ff216, l, tc)
    return x
```

```python
import functools
import math

import jax
import jax.numpy as jnp
from jax import lax
from jax.experimental import pallas as pl
from jax.experimental.pallas import tpu as pltpu

F32 = jnp.float32
BF16 = jnp.bfloat16
EPS = 1e-6
NEG_INF = -1e30
LOG2E = math.log2(math.e)

RET_HEADS = 4
RET_DK = 32
RET_DV = 64
RET_QK = RET_HEADS * RET_DK
RET_V = RET_HEADS * RET_DV
RET_CHUNK = 128
RET_UNROLL = 16
RET_ROT_BASE = 10000.0
HY_CH = 256
HY_EMB = 33
HY_BANDS = 16
HY_ORDER = 64
HY_FAST_DECAY = 0.3
HY_SLOW_DECAY = 1.5
HY_TARGET = 1e-2
ATT_HEADS = 8
ATT_KV_HEADS = 2
ATT_HD = 64
ATT_Q = ATT_HEADS * ATT_HD
ATT_KV = ATT_KV_HEADS * ATT_HD
WINDOW = 128
ATT_BLOCK = 128
ATT_UNROLL = 8
ROPE_BASE = 10000.0
GRID_W = 64

P16_W = 2 * RET_QK + RET_V + ATT_Q + 2 * ATT_KV
P32_W = RET_V + 3 * HY_CH

LANES = 128
VMEM_LIMIT = 56 * 1024 * 1024


def _cparams(sem):
    return pltpu.CompilerParams(dimension_semantics=sem, vmem_limit_bytes=VMEM_LIMIT)


def _const_spec(shape):
    nd = len(shape)
    return pl.BlockSpec(shape, lambda *_: (0,) * nd, pipeline_mode=pl.Buffered(1))


def _layer_spec(stacked, l):
    nd = stacked.ndim - 1
    return pl.BlockSpec((None,) + stacked.shape[1:], lambda *_: (l,) + (0,) * nd,
                        pipeline_mode=pl.Buffered(1))


def _rms(x):
    return x * lax.rsqrt(jnp.mean(x * x, axis=-1, keepdims=True) + EPS)


def _dot(a, b):
    return jnp.dot(a, b, preferred_element_type=F32)


def _dot_nt(a, b):
    return lax.dot_general(a, b, (((1,), (1,)), ((), ())), preferred_element_type=F32)


def _dot_tn(a, b):
    return lax.dot_general(a, b, (((0,), (0,)), ((), ())), preferred_element_type=F32)


def _split(a):
    hi = a.astype(BF16)
    lo = (a - hi.astype(F32)).astype(BF16)
    return hi, lo


def _dot3(a, b):
    ah, al = _split(a)
    bh, bl = _split(b)
    return _dot(ah, bh) + _dot(al, bh) + _dot(ah, bl)


def _ada_kernel(c_ref, w_ref, b_ref, o_ref):
    cv = c_ref[...]
    s = cv * jax.nn.sigmoid(cv)
    o_ref[0] = _dot(s.astype(BF16), w_ref[0].astype(BF16)) + b_ref[0]


def _ada(cc, w_ada, b_ada):
    depth, d, d6 = w_ada.shape
    rows = cc.shape[0]
    tn = 1536
    return pl.pallas_call(
        _ada_kernel,
        grid=(depth, d6 // tn),
        in_specs=[
            pl.BlockSpec((rows, d), lambda l, j: (0, 0)),
            pl.BlockSpec((1, d, tn), lambda l, j: (l, 0, j)),
            pl.BlockSpec((1, 1, tn), lambda l, j: (l, 0, j)),
        ],
        out_specs=pl.BlockSpec((1, rows, tn), lambda l, j: (l, 0, j)),
        out_shape=jax.ShapeDtypeStruct((depth, rows, d6), F32),
        compiler_params=_cparams(("arbitrary", "arbitrary")),
        name="ada",
    )(cc, w_ada, b_ada.reshape(depth, 1, d6))


def _rot(x, cos, sa, sb):
    return x * cos + pltpu.roll(x, LANES - 16, 1) * sa + pltpu.roll(x, 16, 1) * sb


def _in_kernel(x_ref, sh_ref, sc_ref, g_ref, w_ref, *rest, rotary, sub):
    if rotary:
        rc, rsa, rsb, ac, asa, asb, o16_ref, o32_ref = rest
    else:
        o16_ref, o32_ref = rest
    tm = x_ref.shape[1]
    k_scale = RET_DK ** -0.5
    q_scale = ATT_HD ** -0.5 * LOG2E

    def normed(i):
        x = x_ref[0, i * sub:(i + 1) * sub, :]
        return ((_rms(x) * g_ref[...]) * (1.0 + sc_ref[0]) + sh_ref[0]).astype(BF16)

    def project(i, hb):
        r = slice(i * sub, (i + 1) * sub)

        def rot(v, tabs):
            return _rot(v, *[t[r, :] for t in tabs]) if rotary else v

        ret_tabs = (rc, rsa, rsb) if rotary else None
        ax_tabs = (ac, asa, asb) if rotary else None
        pa = _dot(hb, w_ref[:, 1536:2304])
        for j in range(4):
            aq = rot(pa[:, 128 * j:128 * (j + 1)], ax_tabs)
            o16_ref[0, r, 512 + 128 * j:640 + 128 * j] = (aq * q_scale).astype(BF16)
        o16_ref[0, r, 1024:1152] = rot(pa[:, 512:640], ax_tabs).astype(BF16)
        o16_ref[0, r, 1152:1280] = pa[:, 640:768].astype(BF16)
        pr = _dot(hb, w_ref[:, 0:512])
        o16_ref[0, r, 0:128] = rot(pr[:, 0:128], ret_tabs).astype(BF16)
        o16_ref[0, r, 128:256] = (rot(pr[:, 128:256], ret_tabs) * k_scale).astype(BF16)
        o16_ref[0, r, 256:512] = pr[:, 256:512].astype(BF16)
        o32_ref[0, r, :] = _dot(hb, w_ref[:, 512:1536])

    nxt = normed(0)
    for i in range(tm // sub):
        cur = nxt
        if i + 1 < tm // sub:
            nxt = normed(i + 1)
        project(i, cur)


def _in_proj(x3, mods, mod_row, g_pre, w_in16, l, tabs, tm):
    G, R, D = x3.shape
    rotary = tabs is not None
    nt = R // tm
    in_specs = [
        pl.BlockSpec((1, tm, D), lambda g, i: (g, i, 0)),
        pl.BlockSpec((1, 1, D), lambda g, i: (mod_row(g), 0, 0)),
        pl.BlockSpec((1, 1, D), lambda g, i: (mod_row(g), 0, 1)),
        _const_spec((1, D)),
        _layer_spec(w_in16, l),
    ]
    args = [x3, mods, mods, g_pre, w_in16]
    if rotary:
        in_specs += [pl.BlockSpec((tm, LANES), lambda g, i: (i, 0))] * 6
        args += list(tabs)
    return pl.pallas_call(
        functools.partial(_in_kernel, rotary=rotary, sub=min(256, tm)),
        grid=(G, nt),
        in_specs=in_specs,
        out_specs=[
            pl.BlockSpec((1, tm, P16_W), lambda g, i: (g, i, 0)),
            pl.BlockSpec((1, tm, P32_W), lambda g, i: (g, i, 0)),
        ],
        out_shape=[
            jax.ShapeDtypeStruct((G, R, P16_W), BF16),
            jax.ShapeDtypeStruct((G, R, P32_W), F32),
        ],
        compiler_params=_cparams(("parallel", "parallel")),
        name="in_proj_rot" if rotary else "in_proj",
    )(*args)


def _ret_kernel(qkv_ref, rg_ref, dq_ref, ds_ref, dv_ref, g_ref, *rest, L, has_init, want_out):
    rest = list(rest)
    s0_ref = rest.pop(0) if has_init else None
    o_ref = rest.pop(0) if want_out else None
    sfin_ref, st_scr, dec_scr, dmask_scr, cdec_scr = rest
    C = RET_CHUNK
    N = L // C

    @pl.when(pl.program_id(0) == 0)
    def _():
        def log_gamma(ref, r):
            return jnp.log1p(-jnp.exp(ref[r:r + 1, :]))

        lfq, lbq = log_gamma(dq_ref, 0), log_gamma(dq_ref, 1)
        lfs, lbs = log_gamma(ds_ref, 0), log_gamma(ds_ref, 1)
        ri = lax.broadcasted_iota(jnp.int32, (C, LANES), 0).astype(F32)
        dec_scr[0] = jnp.exp(lfq * (ri + 1.0))
        dec_scr[1] = jnp.exp(lfq * (C - 1.0 - ri))
        dec_scr[2] = jnp.exp(lbq * (C - ri))
        dec_scr[3] = jnp.exp(lbq * ri)
        di = lax.broadcasted_iota(jnp.int32, (C, 4 * C), 0)
        dj = lax.broadcasted_iota(jnp.int32, (C, 4 * C), 1) & (C - 1)
        diff = (di - dj).astype(F32)
        dmask_scr[...] = (jnp.where(diff >= 0, jnp.exp(lfs * jnp.maximum(diff, 0.0)), 0.0)
                          + jnp.where(diff <= 0, jnp.exp(lbs * jnp.maximum(-diff, 0.0)), 0.0))
        cdec_scr[0:1, :] = jnp.exp(log_gamma(dv_ref, 0) * float(C))
        cdec_scr[1:2, :] = jnp.exp(log_gamma(dv_ref, 1) * float(C))

    cdec_f, cdec_b = cdec_scr[0:1, :], cdec_scr[1:2, :]

    lane_q = lax.broadcasted_iota(jnp.int32, (1, RET_QK), 1) >> 5
    lane_v = lax.broadcasted_iota(jnp.int32, (1, RET_V), 1) >> 6
    hm = [jnp.where(lane_q == h, 1.0, 0.0).astype(BF16) for h in range(RET_HEADS)]
    cm = [jnp.where(lane_v == h, 1.0, 0.0).astype(BF16) for h in range(RET_HEADS)]
    bd = ((lax.broadcasted_iota(jnp.int32, (RET_QK, RET_V), 0) >> 5)
          == (lax.broadcasted_iota(jnp.int32, (RET_QK, RET_V), 1) >> 6))
    ones64 = jnp.where((lax.broadcasted_iota(jnp.int32, (RET_V, RET_V), 0) >> 6)
                       == (lax.broadcasted_iota(jnp.int32, (RET_V, RET_V), 1) >> 6), 1.0, 0.0).astype(BF16)

    if has_init:
        sf0 = s0_ref[0, 0]
        sb0 = s0_ref[0, 1]
    else:
        sf0 = jnp.zeros((RET_QK, RET_V), F32)
        sb0 = jnp.zeros((RET_QK, RET_V), F32)

    def kv_update(k16, v16, kdec, cdec, s):
        kd = (k16.astype(F32) * kdec).astype(BF16)
        return cdec * s + jnp.where(bd, _dot_tn(kd, v16), 0.0)

    def kv_at(n):
        r0 = pl.multiple_of(n * C, C)
        return qkv_ref[0, pl.ds(r0, C), 128:256], qkv_ref[0, pl.ds(r0, C), 256:512]

    U = min(RET_UNROLL, N)

    def scan_body(it, carry):
        sf, sb = carry
        for u in range(U):
            nf = it * U + u
            nb = N - 1 - nf
            st_scr[nf, 0:RET_QK, :] = sf.astype(BF16)
            sf = kv_update(*kv_at(nf), dec_scr[1], cdec_f, sf)
            st_scr[nb, RET_QK:2 * RET_QK, :] = sb.astype(BF16)
            sb = kv_update(*kv_at(nb), dec_scr[3], cdec_b, sb)
        return sf, sb

    sf_fin, sb_fin = lax.fori_loop(0, N // U, scan_body, (sf0, sb0))
    sfin_ref[0, 0] = sf_fin
    sfin_ref[0, 1] = sb_fin

    if not want_out:
        return

    def out_body(it, carry):
        ns = [it * U + u for u in range(U)]
        rows = [pl.ds(pl.multiple_of(n * C, C), C) for n in ns]
        q16s = [qkv_ref[0, r, 0:128] for r in rows]
        kvs = [kv_at(n) for n in ns]
        scores = [_dot_nt(q16, jnp.concatenate([k16 * hm[h] for h in range(RET_HEADS)], axis=0))
                  for q16, (k16, _) in zip(q16s, kvs)]
        outs = []
        for n, q16, (_, v16), s in zip(ns, q16s, kvs, scores):
            qf = q16.astype(F32)
            vblk = jnp.concatenate([v16 * cm[h] for h in range(RET_HEADS)], axis=0)
            lhs = jnp.concatenate([(s * dmask_scr[...]).astype(BF16), (qf * dec_scr[0]).astype(BF16),
                                   (qf * dec_scr[2]).astype(BF16)], axis=1)
            outs.append(_dot(lhs, jnp.concatenate([vblk, st_scr[n]], axis=0)))
        mss = [_dot((o * o).astype(BF16), ones64) * (1.0 / RET_DV) for o in outs]
        for r, o, ms in zip(rows, outs, mss):
            rg = rg_ref[0, r, :]
            gated = (o * lax.rsqrt(ms + EPS)) * (rg * jax.nn.sigmoid(rg))
            o_ref[0, r, :] = (_rms(gated) * g_ref[...]).astype(BF16)
        return carry

    lax.fori_loop(0, N // U, out_body, 0)


def _retention(p16, p32, dq, ds, dv, g_ret, s0, want_out=True):
    B, L, _ = p16.shape
    has_init = s0 is not None
    in_specs = [
        pl.BlockSpec((1, L, 512), lambda b: (b, 0, 0)),
        pl.BlockSpec((1, L, RET_V), lambda b: (b, 0, 0)),
        _const_spec(dq.shape), _const_spec(ds.shape), _const_spec(dv.shape),
        _const_spec((1, RET_V)),
    ]
    args = [p16, p32, dq, ds, dv, g_ret]
    if has_init:
        in_specs.append(pl.BlockSpec((1, 2, RET_QK, RET_V), lambda b: (b, 0, 0, 0)))
        args.append(s0)
    out_specs = [pl.BlockSpec((1, 2, RET_QK, RET_V), lambda b: (b, 0, 0, 0))]
    out_shape = [jax.ShapeDtypeStruct((B, 2, RET_QK, RET_V), F32)]
    if want_out:
        out_specs.insert(0, pl.BlockSpec((1, L, RET_V), lambda b: (b, 0, 0)))
        out_shape.insert(0, jax.ShapeDtypeStruct((B, L, RET_V), BF16))
    res = pl.pallas_call(
        functools.partial(_ret_kernel, L=L, has_init=has_init, want_out=want_out),
        grid=(B,),
        in_specs=in_specs,
        out_specs=out_specs,
        out_shape=out_shape,
        scratch_shapes=[pltpu.VMEM((L // RET_CHUNK, 2 * RET_QK, RET_V), BF16),
                        pltpu.VMEM((4, RET_CHUNK, LANES), F32),
                        pltpu.VMEM((RET_CHUNK, 4 * RET_CHUNK), F32),
                        pltpu.VMEM((2, RET_V), F32)],
        compiler_params=_cparams(("arbitrary",)),
        name="retention_init" if has_init else "retention",
    )(*args)
    return tuple(res) if want_out else (None, res[0])


def _halves(scr, start, rows):
    return jnp.concatenate([scr[0, pl.ds(start, rows, stride=2), :],
                            scr[1, pl.ds(start, rows, stride=2), :]], axis=1)


def _alt_sign(rows, first=0):
    r = lax.broadcasted_iota(jnp.int32, (rows, HY_CH), 0) + first
    return jnp.where((r & 1) == 1, -1.0, 1.0)


def _filt_kernel(zf_ref, w1_ref, b1_ref, w2_ref, b2_ref, w3_ref, fr_ref, dec_ref,
                 ce_ref, se_ref, co_ref, so_ref, hra_ref, hia_ref, hrb_ref, hib_ref, hm_ref,
                 a_scr, d_scr, *, L):
    n = 2 * L
    M = L // 2
    fr = fr_ref[...]
    h = jnp.sin(fr * (_dot3(zf_ref[...], w1_ref[...]) + b1_ref[...]))
    for j in range(2):
        h = jnp.sin(fr * (_dot3(h, w2_ref[j]) + b2_ref[j]))
    h = _dot3(h, w3_ref[...])
    dec = dec_ref[...]
    row = lax.broadcasted_iota(jnp.int32, (L, HY_CH), 0)
    hf = h[:, 0:HY_CH] * dec
    hb = jnp.where(row == 0, 0.0, h[:, HY_CH:2 * HY_CH] * dec)
    for scr, val in ((a_scr, hf + hb), (d_scr, hf - hb)):
        scr[0] = val[:, 0:LANES]
        scr[1] = val[:, LANES:2 * LANES]
    ae, ao = _halves(a_scr, 0, M), _halves(a_scr, 1, M)
    de, do = _halves(d_scr, 0, M), _halves(d_scr, 1, M)
    sgn = _alt_sign(M)
    hm_ref[0:1, :] = jnp.sum(ae * sgn, axis=0, keepdims=True) * (2.0 / n)
    hm_ref[1:2, :] = jnp.sum(do * sgn, axis=0, keepdims=True) * (-2.0 / n)

    def dft(m_ref, x):
        xh, xl = _split(x)
        return _dot(m_ref[...], xh) + _dot(m_ref[...], xl)

    wk = jnp.where(lax.broadcasted_iota(jnp.int32, (M, 1), 0) == 0, 1.0 / n, 2.0 / n)
    ea, ta = dft(ce_ref, ae), dft(co_ref, ao)
    hra_ref[...] = (ea + ta) * wk
    hrb_ref[...] = (ea - ta) * wk
    ed, td = dft(se_ref, de), dft(so_ref, do)
    hia_ref[...] = (ed + td) * wk
    hib_ref[...] = (td - ed) * wk


def _hyena_spectrum(L, zf, w1, b1, w2, b2, w3, fr, dec, mats):
    M = L // 2
    args = (zf, w1, b1, w2, b2, w3, fr, dec) + tuple(mats[:4])
    half = jax.ShapeDtypeStruct((M, HY_CH), F32)
    return pl.pallas_call(
        functools.partial(_filt_kernel, L=L),
        grid=(1,),
        in_specs=[_const_spec(a.shape) for a in args],
        out_specs=[_const_spec((M, HY_CH))] * 4 + [_const_spec((2, HY_CH))],
        out_shape=[half] * 4 + [jax.ShapeDtypeStruct((2, HY_CH), F32)],
        scratch_shapes=[pltpu.VMEM((2, L, LANES), F32)] * 2,
        compiler_params=_cparams(("arbitrary",)),
        name="hyena_filter",
    )(*args)


def _hy_kernel(v_ref, x1_ref, x0_ref, sw_ref, sb_ref, bias_ref, g_ref,
               ce_ref, se_ref, co_ref, so_ref, cot_ref, sot_ref,
               hra_ref, hia_ref, hrb_ref, hib_ref, hm_ref, o_ref,
               z_scr, x0_scr, out_scr, ger_scr, gei_scr, gor_scr, goi_scr, *, L, R):
    nchunks = L // R
    M = L // 2
    RB = min(256, M)
    for c in range(nchunks):
        r0 = c * R
        lo = max(r0 - 8, 0)
        hi = min(r0 + R + 8, L)
        rows = hi - lo
        off = r0 - lo
        grow = lax.broadcasted_iota(jnp.int32, (R, HY_CH), 0) + r0

        def conv(ref, c0):
            ext = ref[0, lo:hi, :]
            up = pltpu.roll(ext, 1, 0)[off:off + R]
            un = pltpu.roll(ext, rows - 1, 0)[off:off + R]
            if c == 0:
                up = jnp.where(grow == 0, 0.0, up)
            if c == nchunks - 1:
                un = jnp.where(grow == L - 1, 0.0, un)
            u = ref[0, r0:r0 + R, :]
            return (up * sw_ref[0:1, c0:c0 + HY_CH] + u * sw_ref[1:2, c0:c0 + HY_CH]
                    + un * sw_ref[2:3, c0:c0 + HY_CH] + sb_ref[:, c0:c0 + HY_CH])

        z = conv(v_ref, 0) * conv(x1_ref, HY_CH)
        x0 = conv(x0_ref, 2 * HY_CH)
        for hlf in range(2):
            z_scr[hlf, r0:r0 + R, :] = z[:, hlf * LANES:(hlf + 1) * LANES]
            x0_scr[hlf, r0:r0 + R, :] = x0[:, hlf * LANES:(hlf + 1) * LANES]

    ze, zo = _halves(z_scr, 0, M), _halves(z_scr, 1, M)
    sgn = _alt_sign(M)
    mid_r = jnp.sum(ze * sgn, axis=0, keepdims=True)
    mid_i = -jnp.sum(zo * sgn, axis=0, keepdims=True)
    ym_r = mid_r * hm_ref[0:1, :] - mid_i * hm_ref[1:2, :]
    ym_i = mid_r * hm_ref[1:2, :] + mid_i * hm_ref[0:1, :]
    zeb, zob = ze.astype(BF16), zo.astype(BF16)
    for kb in range(M // RB):
        ks = slice(kb * RB, (kb + 1) * RB)
        er, ei = _dot(ce_ref[ks, :], zeb), _dot(se_ref[ks, :], zeb)
        tr, ti = _dot(co_ref[ks, :], zob), _dot(so_ref[ks, :], zob)
        ar, ai, br, bi = er + tr, ei + ti, er - tr, ti - ei
        hra, hia, hrb, hib = hra_ref[ks, :], hia_ref[ks, :], hrb_ref[ks, :], hib_ref[ks, :]
        yar, yai = ar * hra - ai * hia, ar * hia + ai * hra
        ybr, ybi = br * hrb - bi * hib, br * hib + bi * hrb
        ger_scr[ks, :] = (yar + ybr).astype(BF16)
        gei_scr[ks, :] = (yai - ybi).astype(BF16)
        gor_scr[ks, :] = (yar - ybr).astype(BF16)
        goi_scr[ks, :] = (yai + ybi).astype(BF16)

    ger, gei, gor, goi = ger_scr[...], gei_scr[...], gor_scr[...], goi_scr[...]
    for tb in range(M // RB):
        ts = slice(tb * RB, (tb + 1) * RB)
        sg = _alt_sign(RB, tb * RB)
        ys = (_dot(ce_ref[ts, :], ger) + _dot(se_ref[ts, :], gei) + sg * ym_r,
              _dot(cot_ref[ts, :], gor) + _dot(sot_ref[ts, :], goi) - sg * ym_i)
        for par, y in enumerate(ys):
            first = 2 * tb * RB + par
            out = (y + _halves(z_scr, first, RB) * bias_ref[...]) * _halves(x0_scr, first, RB)
            out = _rms(out) * g_ref[...]
            for hlf in range(2):
                out_scr[hlf, pl.ds(first, RB, stride=2), :] = out[:, hlf * LANES:(hlf + 1) * LANES]

    for c in range(nchunks):
        r = slice(c * R, (c + 1) * R)
        o_ref[0, r, :] = jnp.concatenate([out_scr[0, r, :], out_scr[1, r, :]], axis=1).astype(BF16)


def _hyena(p32, short_w, short_b, hy_bias, g_hy, mats, spec):
    B, L, _ = p32.shape
    R = min(256, L)
    M = L // 2
    col = lambda j: pl.BlockSpec((1, L, HY_CH), lambda b: (b, 0, j))
    consts = (short_w, short_b, hy_bias, g_hy) + tuple(mats) + tuple(spec)
    return pl.pallas_call(
        functools.partial(_hy_kernel, L=L, R=R),
        grid=(B,),
        in_specs=[col(1), col(2), col(3)] + [_const_spec(a.shape) for a in consts],
        out_specs=pl.BlockSpec((1, L, HY_CH), lambda b: (b, 0, 0)),
        out_shape=jax.ShapeDtypeStruct((B, L, HY_CH), BF16),
        scratch_shapes=[pltpu.VMEM((2, L, LANES), F32)] * 3 + [pltpu.VMEM((M, HY_CH), BF16)] * 4,
        compiler_params=_cparams(("parallel",)),
        name="hyena",
    )(p32, p32, p32, *consts)


def _expand_kv(kv_ref, k2_scr, vv_scr, rows):
    R = min(256, rows)
    lo = lax.broadcasted_iota(jnp.int32, (R, LANES), 1) < ATT_HD
    ones = jnp.ones((R, LANES), BF16)
    for c in range(rows // R):
        sl = slice(c * R, (c + 1) * R)
        k = kv_ref[0, sl, 0:128].astype(F32)
        v = kv_ref[0, sl, 128:256].astype(F32)
        kr = pltpu.roll(k, ATT_HD, 1)
        vr = pltpu.roll(v, ATT_HD, 1)
        k2_scr[0, sl, :] = jnp.where(lo, k, kr).astype(BF16)
        k2_scr[1, sl, :] = jnp.where(lo, kr, k).astype(BF16)
        vv_scr[0, sl, 0:128] = jnp.where(lo, v, vr).astype(BF16)
        vv_scr[1, sl, 0:128] = jnp.where(lo, vr, v).astype(BF16)
        vv_scr[0, sl, 128:256] = ones
        vv_scr[1, sl, 128:256] = ones


def _att_kernel(sink_ref, q_ref, *rest, L, Lc, band):
    if band:
        (kv_ref, ckv_ref, g_ref, o_ref, k2_scr, vv_scr, ck2_scr, cvv_scr, att_scr) = rest
        _expand_kv(kv_ref, k2_scr, vv_scr, L)
    else:
        (ckv_ref, g_ref, o_ref, ck2_scr, cvv_scr, att_scr) = rest
    _expand_kv(ckv_ref, ck2_scr, cvv_scr, Lc)

    T = ATT_BLOCK
    nb = L // T
    lane = lax.broadcasted_iota(jnp.int32, (1, LANES), 1)
    lo = lane < ATT_HD
    lo16 = jnp.where(lo, 1.0, 0.0).astype(BF16)
    hi16 = jnp.where(lo, 0.0, 1.0).astype(BF16)
    hrow = lax.broadcasted_iota(jnp.int32, (4 * T, 1), 0) >> 7
    ti = lax.broadcasted_iota(jnp.int32, (T, T), 0)
    tj = lax.broadcasted_iota(jnp.int32, (T, T), 1)
    tri_prev = jnp.where(tj >= ti, 0.0, NEG_INF)
    tri_next = jnp.where(tj <= ti, 0.0, NEG_INF)

    def add_bias(s, bias):
        return (s.reshape(4, T, T) + bias[None]).reshape(4 * T, T)

    U = min(ATT_UNROLL, nb)

    def step(it, carry):
        blocks = []
        for u in range(U):
            n = it * U + u
            blk = dict(r0=pl.multiple_of(n * T, T))
            if band:
                blk.update(rp=pl.multiple_of(jnp.maximum(n - 1, 0) * T, T),
                           rn=pl.multiple_of(jnp.minimum(n + 1, nb - 1) * T, T),
                           bias_p=tri_prev + jnp.where(n == 0, NEG_INF, 0.0),
                           bias_n=tri_next + jnp.where(n == nb - 1, NEG_INF, 0.0))
            blocks.append(blk)
        items = [(u, g) for u in range(U) for g in range(ATT_KV_HEADS)]

        def keys_of(scr, cscr, blk, g):
            parts = [cscr[g]]
            if band:
                parts += [scr[g, pl.ds(blk[k], T), :] for k in ("rp", "r0", "rn")]
            return jnp.concatenate(parts, axis=0)

        scores, sinks = [], []
        for u, g in items:
            qb = q_ref[0, pl.ds(blocks[u]["r0"], T), 256 * g:256 * (g + 1)]
            qs = []
            for p in range(2):
                qp = qb[:, 128 * p:128 * (p + 1)]
                qs += [qp * lo16, qp * hi16]
            qst = jnp.concatenate(qs, axis=0)
            scores.append(_dot_nt(qst, keys_of(k2_scr if band else None, ck2_scr, blocks[u], g)))
            sinks.append(LOG2E * jnp.where(
                hrow == 0, sink_ref[4 * g],
                jnp.where(hrow == 1, sink_ref[4 * g + 1],
                          jnp.where(hrow == 2, sink_ref[4 * g + 2], sink_ref[4 * g + 3]))))
        probs, esks = [], []
        for (u, g), s, sk in zip(items, scores, sinks):
            cols = [s[:, j * T:(j + 1) * T] for j in range(s.shape[1] // T)]
            if band:
                cols[-3] = add_bias(cols[-3], blocks[u]["bias_p"])
                cols[-1] = add_bias(cols[-1], blocks[u]["bias_n"])
            mx = cols[0]
            for c_ in cols[1:]:
                mx = jnp.maximum(mx, c_)
            m = jnp.maximum(jnp.max(mx, axis=-1, keepdims=True), sk)
            probs.append(jnp.concatenate([jnp.exp2(c_ - m).astype(BF16) for c_ in cols], axis=1))
            esks.append(jnp.exp2(sk - m))
        for (u, g), pr, esk in zip(items, probs, esks):
            pv = _dot(pr, keys_of(vv_scr if band else None, cvv_scr, blocks[u], g))
            res = pv[:, 0:LANES] / (pv[:, LANES:2 * LANES] + esk)
            for p in range(2):
                c0 = 128 * (2 * g + p)
                att_scr[u, :, c0:c0 + 128] = jnp.where(lo, res[2 * p * T:(2 * p + 1) * T],
                                                       res[(2 * p + 1) * T:(2 * p + 2) * T])
        for u in range(U):
            o_ref[0, pl.ds(blocks[u]["r0"], T), :] = (_rms(att_scr[u]) * g_ref[...]).astype(BF16)
        return carry

    lax.fori_loop(0, nb // U, step, 0)


def _attention(sink, p16, pc16, g_att, band):
    B, Lc, _ = pc16.shape
    L = p16.shape[1] if band else Lc
    qsrc = p16 if band else pc16
    kv_spec = lambda n: pl.BlockSpec((1, n, 256), lambda b: (b, 0, 4))
    in_specs = [pl.BlockSpec(memory_space=pltpu.SMEM),
                pl.BlockSpec((1, L, ATT_Q), lambda b: (b, 0, 1))]
    args = [sink, qsrc]
    scratch = []
    if band:
        in_specs.append(kv_spec(L))
        args.append(p16)
        scratch += [pltpu.VMEM((2, L, LANES), BF16), pltpu.VMEM((2, L, 2 * LANES), BF16)]
    in_specs += [kv_spec(Lc), _const_spec((1, ATT_Q))]
    args += [pc16, g_att]
    scratch += [pltpu.VMEM((2, Lc, LANES), BF16), pltpu.VMEM((2, Lc, 2 * LANES), BF16),
                pltpu.VMEM((ATT_UNROLL, ATT_BLOCK, ATT_Q), F32)]
    return pl.pallas_call(
        functools.partial(_att_kernel, L=L, Lc=Lc, band=band),
        grid=(B,),
        in_specs=in_specs,
        out_specs=pl.BlockSpec((1, L, ATT_Q), lambda b: (b, 0, 0)),
        out_shape=jax.ShapeDtypeStruct((B, L, ATT_Q), BF16),
        scratch_shapes=scratch,
        compiler_params=_cparams(("parallel",)),
        name="attn_window" if band else "attn_ctx",
    )(*args)


def _out_mlp_kernel(x_ref, nr_ref, nh_ref, na_ref, gta_ref, shm_ref, scm_ref, gtm_ref,
                    gpm_ref, gpre_ref, gpost_ref, wo_ref, w1_ref, w2_ref, o_ref, *, ff_chunk, sub):
    tm = x_ref.shape[1]
    dff = w1_ref.shape[1]

    def mixer_out(i):
        r = slice(i * sub, (i + 1) * sub)
        mix = (_dot(nr_ref[0, r, :], wo_ref[0:256, :]) + _dot(nh_ref[0, r, :], wo_ref[256:512, :])
               + _dot(na_ref[0, r, :], wo_ref[512:1024, :]))
        x1 = x_ref[0, r, :] + gta_ref[0] * (_rms(mix) * gpm_ref[...])
        h = ((_rms(x1) * gpre_ref[...]) * (1.0 + scm_ref[0]) + shm_ref[0]).astype(BF16)
        return x1, h

    def mlp(i, x1, h):
        acc = None
        for j in range(dff // ff_chunk):
            sl = slice(j * ff_chunk, (j + 1) * ff_chunk)
            hj = jnp.square(jnp.maximum(_dot(h, w1_ref[:, sl]), 0.0)).astype(BF16)
            part = _dot(hj, w2_ref[sl, :])
            acc = part if acc is None else acc + part
        o_ref[0, i * sub:(i + 1) * sub, :] = x1 + gtm_ref[0] * (_rms(acc) * gpost_ref[...])

    nxt = mixer_out(0)
    for i in range(tm // sub):
        cur = nxt
        if i + 1 < tm // sub:
            nxt = mixer_out(i + 1)
        mlp(i, *cur)


def _out_mlp(x3, nr, nh, na, mods, mod_row, g_post_mix, g_pre_mlp, g_post_mlp, wo16, w116, w216, l, tm):
    G, R, D = x3.shape
    row = lambda w: pl.BlockSpec((1, tm, w), lambda g, i: (g, i, 0))
    mod = lambda j: pl.BlockSpec((1, 1, D), lambda g, i: (mod_row(g), 0, j))
    return pl.pallas_call(
        functools.partial(_out_mlp_kernel, ff_chunk=1024, sub=min(512, tm)),
        grid=(G, R // tm),
        in_specs=[row(D), row(RET_V), row(HY_CH), row(ATT_Q), mod(2), mod(3), mod(4), mod(5),
                  _const_spec((1, D)), _const_spec((1, D)), _const_spec((1, D)),
                  _layer_spec(wo16, l), _layer_spec(w116, l), _layer_spec(w216, l)],
        out_specs=row(D),
        out_shape=jax.ShapeDtypeStruct((G, R, D), F32),
        compiler_params=_cparams(("parallel", "parallel")),
        name="out_mlp",
    )(x3, nr, nh, na, mods, mods, mods, mods, g_post_mix, g_pre_mlp, g_post_mlp, wo16, w116, w216)


def _rot_tables(cos, sin):
    lane = jnp.arange(LANES) % 32
    first = (lane < 16)[None, :]
    return cos, jnp.where(first, -sin, 0.0), jnp.where(first, 0.0, sin)


def _rotary_tables(L):
    pos = jnp.arange(L)
    row = (pos // GRID_W).astype(F32)
    col = (pos % GRID_W).astype(F32)
    half = ATT_HD // 2
    inv_ax = ROPE_BASE ** (-jnp.arange(0, half, 2, dtype=F32) / half)

    def tab(p, inv):
        ang = p[:, None] * inv[None, :]
        ang = jnp.concatenate([ang, ang], axis=-1)
        return jnp.cos(ang), jnp.sin(ang)

    cr, sr = tab(row, inv_ax)
    cc, sc = tab(col, inv_ax)
    inv_ret = 1.0 / (RET_ROT_BASE ** jnp.linspace(0.0, 1.0, RET_DK // 2, dtype=F32))
    ct, st = tab(pos.astype(F32), inv_ret)
    ret = _rot_tables(jnp.tile(ct, (1, 4)), jnp.tile(st, (1, 4)))
    ax = _rot_tables(jnp.tile(jnp.concatenate([cr, cc], -1), (1, 2)),
                     jnp.tile(jnp.concatenate([sr, sc], -1), (1, 2)))
    return ret + ax


def _dft_mats(L):
    n = 2 * L
    k = jnp.arange(L // 2, dtype=jnp.int32)

    def pair(t):
        ang = ((k[:, None] * t[None, :]) % n).astype(F32) * (2.0 * math.pi / n)
        return jnp.cos(ang).astype(BF16), (-jnp.sin(ang)).astype(BF16)

    ce, se = pair(2 * k)
    co, so = pair(2 * k + 1)
    return ce, se, co, so, co.T, so.T


def _filter_features(L):
    t = jnp.linspace(0.0, 1.0, L, dtype=F32)[:, None]
    w = 2.0 * math.pi * jnp.arange(L, dtype=F32) / L
    bands = jnp.linspace(1e-4, HY_BANDS - 1, HY_BANDS, dtype=F32)
    ang = w[:, None] * bands[None, :]
    z = jnp.concatenate([t, jnp.cos(ang), -jnp.sin(ang)], axis=-1)
    zf = jnp.pad(z, ((0, 0), (0, LANES - HY_EMB)))
    max_decay = math.log(HY_TARGET) / HY_FAST_DECAY
    min_decay = math.log(HY_TARGET) / HY_SLOW_DECAY
    deltas = jnp.linspace(min_decay, max_decay, HY_CH, dtype=F32)
    dec = jnp.exp(-t * jnp.abs(deltas)[None, :])
    return zf, dec


def _row_tile(rows):
    return 1024 if rows % 1024 == 0 else 512


def _pad_to(a, shape):
    return jnp.pad(a, [(0, s - d) for d, s in zip(a.shape, shape)])


def kernel(x, c, ctx, c_ctx, w_ada, b_ada, g_pre_mix, g_post_mix, g_pre_mlp, g_post_mlp, w_in,
           ret_decay_fwd, ret_decay_bwd, hy_short_w, hy_short_b, hy_f_w1, hy_f_b1, hy_f_w2, hy_f_b2,
           hy_f_w3, hy_f_freq, hy_bias, attn_sink, g_ret, g_hy, g_att, w_out, w_ff1, w_ff2):
    B, L, D = x.shape
    Lc = ctx.shape[1]
    depth = w_ada.shape[0]
    assert D == 1024 and w_in.shape[2] == 2304 and L % 256 == 0 and Lc % 256 == 0 and L >= 3 * ATT_BLOCK

    rows = -(-(B + 1) // 8) * 8
    cc = _pad_to(jnp.concatenate([c, c_ctx[None, :]], axis=0), (rows, D))
    mods_all = _ada(cc, w_ada, b_ada).reshape(depth, rows, 1, 6 * D)

    w_in16 = w_in.astype(BF16)
    w_out16 = w_out.astype(BF16)
    w_ff116 = w_ff1.astype(BF16)
    w_ff216 = w_ff2.astype(BF16)

    tabs = _rotary_tables(L)
    mats = {n: _dft_mats(n) for n in {L, Lc}}
    feats = {n: _filter_features(n) for n in {L, Lc}}

    ctx_rows = B * Lc
    tm = _row_tile(L)
    tc = _row_tile(ctx_rows)
    xc = ctx.reshape(ctx_rows // tc, tc, D)
    lat_row = lambda g: g
    ctx_row = lambda g: B

    for l in range(depth):
        last = l == depth - 1
        mods = mods_all[l]
        dec2 = jnp.stack([ret_decay_fwd[l], ret_decay_bwd[l]])
        dq = jnp.repeat(dec2, RET_DK, axis=1)
        ds = jnp.repeat(dec2, RET_CHUNK, axis=1)
        dv = jnp.repeat(dec2, RET_DV, axis=1)
        gr, gh, ga = g_ret[l][None], g_hy[l][None], g_att[l][None]
        gpre, gpm = g_pre_mix[l][None], g_post_mix[l][None]
        gprm, gpom = g_pre_mlp[l][None], g_post_mlp[l][None]
        fw1 = _pad_to(hy_f_w1[l], (LANES, LANES))
        fb1 = _pad_to(hy_f_b1[l][None], (1, LANES))
        fw2 = _pad_to(hy_f_w2[l], (2, LANES, LANES))
        fb2 = _pad_to(hy_f_b2[l][:, None, :], (2, 1, LANES))
        fw3 = _pad_to(hy_f_w3[l], (LANES, 2 * HY_CH))
        ffr = _pad_to(hy_f_freq[l][None], (1, LANES))
        sw, sb, hbias = hy_short_w[l], hy_short_b[l][None], hy_bias[l][None]

        def spectrum(n):
            zf, dec = feats[n]
            return _hyena_spectrum(n, zf, fw1, fb1, fw2, fb2, fw3, ffr, dec, mats[n])

        pc16, pc32 = _in_proj(xc, mods, ctx_row, gpre, w_in16, l, None, tc)
        pc16 = pc16.reshape(B, Lc, P16_W)
        pc32 = pc32.reshape(B, Lc, P32_W)
        cret, cstate = _retention(pc16, pc32, dq, ds, dv, gr, None, want_out=not last)

        p16, p32 = _in_proj(x, mods, lat_row, gpre, w_in16, l, tabs, tm)
        ret, _ = _retention(p16, p32, dq, ds, dv, gr, cstate)
        hyo = _hyena(p32, sw, sb, hbias, gh, mats[L], spectrum(L))
        att = _attention(attn_sink[l], p16, pc16, ga, True)
        x = _out_mlp(x, ret, hyo, att, mods, lat_row, gpm, gprm, gpom,
                     w_out16, w_ff116, w_ff216, l, tm)

        if not last:
            chyo = _hyena(pc32, sw, sb, hbias, gh, mats[Lc], spectrum(Lc))
            catt = _attention(attn_sink[l], None, pc16, ga, False)
            r3 = lambda a: a.reshape(ctx_rows // tc, tc, a.shape[-1])
            xc = _out_mlp(xc, r3(cret), r3(chyo), r3(catt), mods, ctx_row, gpm, gprm, gpom,
                          w_out16, w_ff116, w_ff216, l, tc)
    return x
```

```python
import functools
import math

import jax
import jax.numpy as jnp
from jax import lax
from jax.experimental import pallas as pl
from jax.experimental.pallas import tpu as pltpu

F32 = jnp.float32
BF16 = jnp.bfloat16
EPS = 1e-6
NEG_INF = -1e30
LOG2E = math.log2(math.e)

RET_HEADS = 4
RET_DK = 32
RET_DV = 64
RET_QK = RET_HEADS * RET_DK
RET_V = RET_HEADS * RET_DV
RET_CHUNK = 128
RET_UNROLL = 16
RET_ROT_BASE = 10000.0
HY_CH = 256
HY_EMB = 33
HY_BANDS = 16
HY_ORDER = 64
HY_FAST_DECAY = 0.3
HY_SLOW_DECAY = 1.5
HY_TARGET = 1e-2
ATT_HEADS = 8
ATT_KV_HEADS = 2
ATT_HD = 64
ATT_Q = ATT_HEADS * ATT_HD
ATT_KV = ATT_KV_HEADS * ATT_HD
WINDOW = 128
ATT_BLOCK = 128
ATT_UNROLL = 8
ROPE_BASE = 10000.0
GRID_W = 64

P16_W = 2 * RET_QK + RET_V + ATT_Q + 2 * ATT_KV
P32_W = RET_V + 3 * HY_CH

LANES = 128
VMEM_LIMIT = 56 * 1024 * 1024


def _cparams(sem):
    return pltpu.CompilerParams(dimension_semantics=sem, vmem_limit_bytes=VMEM_LIMIT)


def _const_spec(shape):
    nd = len(shape)
    return pl.BlockSpec(shape, lambda *_: (0,) * nd, pipeline_mode=pl.Buffered(1))


def _layer_spec(stacked, l):
    nd = stacked.ndim - 1
    return pl.BlockSpec((None,) + stacked.shape[1:], lambda *_: (l,) + (0,) * nd,
                        pipeline_mode=pl.Buffered(1))


def _rms(x):
    return x * lax.rsqrt(jnp.mean(x * x, axis=-1, keepdims=True) + EPS)


def _dot(a, b):
    return jnp.dot(a, b, preferred_element_type=F32)


def _dot_nt(a, b):
    return lax.dot_general(a, b, (((1,), (1,)), ((), ())), preferred_element_type=F32)


def _dot_tn(a, b):
    return lax.dot_general(a, b, (((0,), (0,)), ((), ())), preferred_element_type=F32)


def _split(a):
    hi = a.astype(BF16)
    lo = (a - hi.astype(F32)).astype(BF16)
    return hi, lo


def _dot3(a, b):
    ah, al = _split(a)
    bh, bl = _split(b)
    return _dot(ah, bh) + _dot(al, bh) + _dot(ah, bl)


def _ada_kernel(c_ref, w_ref, b_ref, o_ref):
    cv = c_ref[...]
    s = cv * jax.nn.sigmoid(cv)
    o_ref[0] = _dot(s.astype(BF16), w_ref[0].astype(BF16)) + b_ref[0]


def _ada(cc, w_ada, b_ada):
    depth, d, d6 = w_ada.shape
    rows = cc.shape[0]
    tn = 1536
    return pl.pallas_call(
        _ada_kernel,
        grid=(depth, d6 // tn),
        in_specs=[
            pl.BlockSpec((rows, d), lambda l, j: (0, 0)),
            pl.BlockSpec((1, d, tn), lambda l, j: (l, 0, j)),
            pl.BlockSpec((1, 1, tn), lambda l, j: (l, 0, j)),
        ],
        out_specs=pl.BlockSpec((1, rows, tn), lambda l, j: (l, 0, j)),
        out_shape=jax.ShapeDtypeStruct((depth, rows, d6), F32),
        compiler_params=_cparams(("arbitrary", "arbitrary")),
        name="ada",
    )(cc, w_ada, b_ada.reshape(depth, 1, d6))


def _rot(x, cos, sa, sb):
    return x * cos + pltpu.roll(x, LANES - 16, 1) * sa + pltpu.roll(x, 16, 1) * sb


def _in_kernel(x_ref, sh_ref, sc_ref, g_ref, w_ref, *rest, rotary, sub):
    if rotary:
        rc, rsa, rsb, ac, asa, asb, o16_ref, o32_ref = rest
    else:
        o16_ref, o32_ref = rest
    tm = x_ref.shape[1]
    k_scale = RET_DK ** -0.5
    q_scale = ATT_HD ** -0.5 * LOG2E

    def normed(i):
        x = x_ref[0, i * sub:(i + 1) * sub, :]
        return ((_rms(x) * g_ref[...]) * (1.0 + sc_ref[0]) + sh_ref[0]).astype(BF16)

    def project(i, hb):
        r = slice(i * sub, (i + 1) * sub)

        def rot(v, tabs):
            return _rot(v, *[t[r, :] for t in tabs]) if rotary else v

        ret_tabs = (rc, rsa, rsb) if rotary else None
        ax_tabs = (ac, asa, asb) if rotary else None
        pa = _dot(hb, w_ref[:, 1536:2304])
        for j in range(4):
            aq = rot(pa[:, 128 * j:128 * (j + 1)], ax_tabs)
            o16_ref[0, r, 512 + 128 * j:640 + 128 * j] = (aq * q_scale).astype(BF16)
        o16_ref[0, r, 1024:1152] = rot(pa[:, 512:640], ax_tabs).astype(BF16)
        o16_ref[0, r, 1152:1280] = pa[:, 640:768].astype(BF16)
        pr = _dot(hb, w_ref[:, 0:512])
        o16_ref[0, r, 0:128] = rot(pr[:, 0:128], ret_tabs).astype(BF16)
        o16_ref[0, r, 128:256] = (rot(pr[:, 128:256], ret_tabs) * k_scale).astype(BF16)
        o16_ref[0, r, 256:512] = pr[:, 256:512].astype(BF16)
        o32_ref[0, r, :] = _dot(hb, w_ref[:, 512:1536])

    nxt = normed(0)
    for i in range(tm // sub):
        cur = nxt
        if i + 1 < tm // sub:
            nxt = normed(i + 1)
        project(i, cur)


def _in_proj(x3, mods, mod_row, g_pre, w_in16, l, tabs, tm):
    G, R, D = x3.shape
    rotary = tabs is not None
    nt = R // tm
    in_specs = [
        pl.BlockSpec((1, tm, D), lambda g, i: (g, i, 0)),
        pl.BlockSpec((1, 1, D), lambda g, i: (mod_row(g), 0, 0)),
        pl.BlockSpec((1, 1, D), lambda g, i: (mod_row(g), 0, 1)),
        _const_spec((1, D)),
        _layer_spec(w_in16, l),
    ]
    args = [x3, mods, mods, g_pre, w_in16]
    if rotary:
        in_specs += [pl.BlockSpec((tm, LANES), lambda g, i: (i, 0))] * 6
        args += list(tabs)
    return pl.pallas_call(
        functools.partial(_in_kernel, rotary=rotary, sub=min(256, tm)),
        grid=(G, nt),
        in_specs=in_specs,
        out_specs=[
            pl.BlockSpec((1, tm, P16_W), lambda g, i: (g, i, 0)),
            pl.BlockSpec((1, tm, P32_W), lambda g, i: (g, i, 0)),
        ],
        out_shape=[
            jax.ShapeDtypeStruct((G, R, P16_W), BF16),
            jax.ShapeDtypeStruct((G, R, P32_W), F32),
        ],
        compiler_params=_cparams(("parallel", "parallel")),
        name="in_proj_rot" if rotary else "in_proj",
    )(*args)


def _ret_kernel(qkv_ref, rg_ref, dq_ref, ds_ref, dv_ref, g_ref, *rest, L, has_init, want_out):
    rest = list(rest)
    s0_ref = rest.pop(0) if has_init else None
    o_ref = rest.pop(0) if want_out else None
    sfin_ref, st_scr, dec_scr, dmask_scr, cdec_scr = rest
    C = RET_CHUNK
    N = L // C

    @pl.when(pl.program_id(0) == 0)
    def _():
        def log_gamma(ref, r):
            return jnp.log1p(-jnp.exp(ref[r:r + 1, :]))

        lfq, lbq = log_gamma(dq_ref, 0), log_gamma(dq_ref, 1)
        lfs, lbs = log_gamma(ds_ref, 0), log_gamma(ds_ref, 1)
        ri = lax.broadcasted_iota(jnp.int32, (C, LANES), 0).astype(F32)
        dec_scr[0] = jnp.exp(lfq * (ri + 1.0))
        dec_scr[1] = jnp.exp(lfq * (C - 1.0 - ri))
        dec_scr[2] = jnp.exp(lbq * (C - ri))
        dec_scr[3] = jnp.exp(lbq * ri)
        di = lax.broadcasted_iota(jnp.int32, (C, 4 * C), 0)
        dj = lax.broadcasted_iota(jnp.int32, (C, 4 * C), 1) & (C - 1)
        diff = (di - dj).astype(F32)
        dmask_scr[...] = (jnp.where(diff >= 0, jnp.exp(lfs * jnp.maximum(diff, 0.0)), 0.0)
                          + jnp.where(diff <= 0, jnp.exp(lbs * jnp.maximum(-diff, 0.0)), 0.0))
        cdec_scr[0:1, :] = jnp.exp(log_gamma(dv_ref, 0) * float(C))
        cdec_scr[1:2, :] = jnp.exp(log_gamma(dv_ref, 1) * float(C))

    cdec_f, cdec_b = cdec_scr[0:1, :], cdec_scr[1:2, :]

    lane_q = lax.broadcasted_iota(jnp.int32, (1, RET_QK), 1) >> 5
    lane_v = lax.broadcasted_iota(jnp.int32, (1, RET_V), 1) >> 6
    hm = [jnp.where(lane_q == h, 1.0, 0.0).astype(BF16) for h in range(RET_HEADS)]
    cm = [jnp.where(lane_v == h, 1.0, 0.0).astype(BF16) for h in range(RET_HEADS)]
    bd = ((lax.broadcasted_iota(jnp.int32, (RET_QK, RET_V), 0) >> 5)
          == (lax.broadcasted_iota(jnp.int32, (RET_QK, RET_V), 1) >> 6))
    ones64 = jnp.where((lax.broadcasted_iota(jnp.int32, (RET_V, RET_V), 0) >> 6)
                       == (lax.broadcasted_iota(jnp.int32, (RET_V, RET_V), 1) >> 6), 1.0, 0.0).astype(BF16)

    if has_init:
        sf0 = s0_ref[0, 0]
        sb0 = s0_ref[0, 1]
    else:
        sf0 = jnp.zeros((RET_QK, RET_V), F32)
        sb0 = jnp.zeros((RET_QK, RET_V), F32)

    def kv_update(k16, v16, kdec, cdec, s):
        kd = (k16.astype(F32) * kdec).astype(BF16)
        return cdec * s + jnp.where(bd, _dot_tn(kd, v16), 0.0)

    def kv_at(n):
        r0 = pl.multiple_of(n * C, C)
        return qkv_ref[0, pl.ds(r0, C), 128:256], qkv_ref[0, pl.ds(r0, C), 256:512]

    U = min(RET_UNROLL, N)

    def scan_body(it, carry):
        sf, sb = carry
        for u in range(U):
            nf = it * U + u
            nb = N - 1 - nf
            st_scr[nf, 0:RET_QK, :] = sf.astype(BF16)
            sf = kv_update(*kv_at(nf), dec_scr[1], cdec_f, sf)
            st_scr[nb, RET_QK:2 * RET_QK, :] = sb.astype(BF16)
            sb = kv_update(*kv_at(nb), dec_scr[3], cdec_b, sb)
        return sf, sb

    sf_fin, sb_fin = lax.fori_loop(0, N // U, scan_body, (sf0, sb0))
    sfin_ref[0, 0] = sf_fin
    sfin_ref[0, 1] = sb_fin

    if not want_out:
        return

    def out_body(it, carry):
        ns = [it * U + u for u in range(U)]
        rows = [pl.ds(pl.multiple_of(n * C, C), C) for n in ns]
        q16s = [qkv_ref[0, r, 0:128] for r in rows]
        kvs = [kv_at(n) for n in ns]
        scores = [_dot_nt(q16, jnp.concatenate([k16 * hm[h] for h in range(RET_HEADS)], axis=0))
                  for q16, (k16, _) in zip(q16s, kvs)]
        outs = []
        for n, q16, (_, v16), s in zip(ns, q16s, kvs, scores):
            qf = q16.astype(F32)
            vblk = jnp.concatenate([v16 * cm[h] for h in range(RET_HEADS)], axis=0)
            lhs = jnp.concatenate([(s * dmask_scr[...]).astype(BF16), (qf * dec_scr[0]).astype(BF16),
                                   (qf * dec_scr[2]).astype(BF16)], axis=1)
            outs.append(_dot(lhs, jnp.concatenate([vblk, st_scr[n]], axis=0)))
        mss = [_dot((o * o).astype(BF16), ones64) * (1.0 / RET_DV) for o in outs]
        for r, o, ms in zip(rows, outs, mss):
            rg = rg_ref[0, r, :]
            gated = (o * lax.rsqrt(ms + EPS)) * (rg * jax.nn.sigmoid(rg))
            o_ref[0, r, :] = (_rms(gated) * g_ref[...]).astype(BF16)
        return carry

    lax.fori_loop(0, N // U, out_body, 0)


def _retention(p16, p32, dq, ds, dv, g_ret, s0, want_out=True):
    B, L, _ = p16.shape
    has_init = s0 is not None
    in_specs = [
        pl.BlockSpec((1, L, 512), lambda b: (b, 0, 0)),
        pl.BlockSpec((1, L, RET_V), lambda b: (b, 0, 0)),
        _const_spec(dq.shape), _const_spec(ds.shape), _const_spec(dv.shape),
        _const_spec((1, RET_V)),
    ]
    args = [p16, p32, dq, ds, dv, g_ret]
    if has_init:
        in_specs.append(pl.BlockSpec((1, 2, RET_QK, RET_V), lambda b: (b, 0, 0, 0)))
        args.append(s0)
    out_specs = [pl.BlockSpec((1, 2, RET_QK, RET_V), lambda b: (b, 0, 0, 0))]
    out_shape = [jax.ShapeDtypeStruct((B, 2, RET_QK, RET_V), F32)]
    if want_out:
        out_specs.insert(0, pl.BlockSpec((1, L, RET_V), lambda b: (b, 0, 0)))
        out_shape.insert(0, jax.ShapeDtypeStruct((B, L, RET_V), BF16))
    res = pl.pallas_call(
        functools.partial(_ret_kernel, L=L, has_init=has_init, want_out=want_out),
        grid=(B,),
        in_specs=in_specs,
        out_specs=out_specs,
        out_shape=out_shape,
        scratch_shapes=[pltpu.VMEM((L // RET_CHUNK, 2 * RET_QK, RET_V), BF16),
                        pltpu.VMEM((4, RET_CHUNK, LANES), F32),
                        pltpu.VMEM((RET_CHUNK, 4 * RET_CHUNK), F32),
                        pltpu.VMEM((2, RET_V), F32)],
        compiler_params=_cparams(("arbitrary",)),
        name="retention_init" if has_init else "retention",
    )(*args)
    return tuple(res) if want_out else (None, res[0])


def _halves(scr, start, rows):
    return jnp.concatenate([scr[0, pl.ds(start, rows, stride=2), :],
                            scr[1, pl.ds(start, rows, stride=2), :]], axis=1)


def _alt_sign(rows, first=0):
    r = lax.broadcasted_iota(jnp.int32, (rows, HY_CH), 0) + first
    return jnp.where((r & 1) == 1, -1.0, 1.0)


def _filt_kernel(zf_ref, w1_ref, b1_ref, w2_ref, b2_ref, w3_ref, fr_ref, dec_ref,
                 ce_ref, se_ref, co_ref, so_ref, hra_ref, hia_ref, hrb_ref, hib_ref, hm_ref,
                 a_scr, d_scr, *, L):
    n = 2 * L
    M = L // 2
    fr = fr_ref[...]
    h = jnp.sin(fr * (_dot3(zf_ref[...], w1_ref[...]) + b1_ref[...]))
    for j in range(2):
        h = jnp.sin(fr * (_dot3(h, w2_ref[j]) + b2_ref[j]))
    h = _dot3(h, w3_ref[...])
    dec = dec_ref[...]
    row = lax.broadcasted_iota(jnp.int32, (L, HY_CH), 0)
    hf = h[:, 0:HY_CH] * dec
    hb = jnp.where(row == 0, 0.0, h[:, HY_CH:2 * HY_CH] * dec)
    for scr, val in ((a_scr, hf + hb), (d_scr, hf - hb)):
        scr[0] = val[:, 0:LANES]
        scr[1] = val[:, LANES:2 * LANES]
    ae, ao = _halves(a_scr, 0, M), _halves(a_scr, 1, M)
    de, do = _halves(d_scr, 0, M), _halves(d_scr, 1, M)
    sgn = _alt_sign(M)
    hm_ref[0:1, :] = jnp.sum(ae * sgn, axis=0, keepdims=True) * (2.0 / n)
    hm_ref[1:2, :] = jnp.sum(do * sgn, axis=0, keepdims=True) * (-2.0 / n)

    def dft(m_ref, x):
        xh, xl = _split(x)
        return _dot(m_ref[...], xh) + _dot(m_ref[...], xl)

    wk = jnp.where(lax.broadcasted_iota(jnp.int32, (M, 1), 0) == 0, 1.0 / n, 2.0 / n)
    ea, ta = dft(ce_ref, ae), dft(co_ref, ao)
    hra_ref[...] = (ea + ta) * wk
    hrb_ref[...] = (ea - ta) * wk
    ed, td = dft(se_ref, de), dft(so_ref, do)
    hia_ref[...] = (ed + td) * wk
    hib_ref[...] = (td - ed) * wk


def _hyena_spectrum(L, zf, w1, b1, w2, b2, w3, fr, dec, mats):
    M = L // 2
    args = (zf, w1, b1, w2, b2, w3, fr, dec) + tuple(mats[:4])
    half = jax.ShapeDtypeStruct((M, HY_CH), F32)
    return pl.pallas_call(
        functools.partial(_filt_kernel, L=L),
        grid=(1,),
        in_specs=[_const_spec(a.shape) for a in args],
        out_specs=[_const_spec((M, HY_CH))] * 4 + [_const_spec((2, HY_CH))],
        out_shape=[half] * 4 + [jax.ShapeDtypeStruct((2, HY_CH), F32)],
        scratch_shapes=[pltpu.VMEM((2, L, LANES), F32)] * 2,
        compiler_params=_cparams(("arbitrary",)),
        name="hyena_filter",
    )(*args)


def _hy_kernel(v_ref, x1_ref, x0_ref, sw_ref, sb_ref, bias_ref, g_ref,
               ce_ref, se_ref, co_ref, so_ref, cot_ref, sot_ref,
               hra_ref, hia_ref, hrb_ref, hib_ref, hm_ref, o_ref,
               z_scr, x0_scr, out_scr, ger_scr, gei_scr, gor_scr, goi_scr, *, L, R):
    nchunks = L // R
    M = L // 2
    RB = min(256, M)
    for c in range(nchunks):
        r0 = c * R
        lo = max(r0 - 8, 0)
        hi = min(r0 + R + 8, L)
        rows = hi - lo
        off = r0 - lo
        grow = lax.broadcasted_iota(jnp.int32, (R, HY_CH), 0) + r0

        def conv(ref, c0):
            ext = ref[0, lo:hi, :]
            up = pltpu.roll(ext, 1, 0)[off:off + R]
            un = pltpu.roll(ext, rows - 1, 0)[off:off + R]
            if c == 0:
                up = jnp.where(grow == 0, 0.0, up)
            if c == nchunks - 1:
                un = jnp.where(grow == L - 1, 0.0, un)
            u = ref[0, r0:r0 + R, :]
            return (up * sw_ref[0:1, c0:c0 + HY_CH] + u * sw_ref[1:2, c0:c0 + HY_CH]
                    + un * sw_ref[2:3, c0:c0 + HY_CH] + sb_ref[:, c0:c0 + HY_CH])

        z = conv(v_ref, 0) * conv(x1_ref, HY_CH)
        x0 = conv(x0_ref, 2 * HY_CH)
        for hlf in range(2):
            z_scr[hlf, r0:r0 + R, :] = z[:, hlf * LANES:(hlf + 1) * LANES]
            x0_scr[hlf, r0:r0 + R, :] = x0[:, hlf * LANES:(hlf + 1) * LANES]

    ze, zo = _halves(z_scr, 0, M), _halves(z_scr, 1, M)
    sgn = _alt_sign(M)
    mid_r = jnp.sum(ze * sgn, axis=0, keepdims=True)
    mid_i = -jnp.sum(zo * sgn, axis=0, keepdims=True)
    ym_r = mid_r * hm_ref[0:1, :] - mid_i * hm_ref[1:2, :]
    ym_i = mid_r * hm_ref[1:2, :] + mid_i * hm_ref[0:1, :]
    zeb, zob = ze.astype(BF16), zo.astype(BF16)
    for kb in range(M // RB):
        ks = slice(kb * RB, (kb + 1) * RB)
        er, ei = _dot(ce_ref[ks, :], zeb), _dot(se_ref[ks, :], zeb)
        tr, ti = _dot(co_ref[ks, :], zob), _dot(so_ref[ks, :], zob)
        ar, ai, br, bi = er + tr, ei + ti, er - tr, ti - ei
        hra, hia, hrb, hib = hra_ref[ks, :], hia_ref[ks, :], hrb_ref[ks, :], hib_ref[ks, :]
        yar, yai = ar * hra - ai * hia, ar * hia + ai * hra
        ybr, ybi = br * hrb - bi * hib, br * hib + bi * hrb
        ger_scr[ks, :] = (yar + ybr).astype(BF16)
        gei_scr[ks, :] = (yai - ybi).astype(BF16)
        gor_scr[ks, :] = (yar - ybr).astype(BF16)
        goi_scr[ks, :] = (yai + ybi).astype(BF16)

    ger, gei, gor, goi = ger_scr[...], gei_scr[...], gor_scr[...], goi_scr[...]
    for tb in range(M // RB):
        ts = slice(tb * RB, (tb + 1) * RB)
        sg = _alt_sign(RB, tb * RB)
        ys = (_dot(ce_ref[ts, :], ger) + _dot(se_ref[ts, :], gei) + sg * ym_r,
              _dot(cot_ref[ts, :], gor) + _dot(sot_ref[ts, :], goi) - sg * ym_i)
        for par, y in enumerate(ys):
            first = 2 * tb * RB + par
            out = (y + _halves(z_scr, first, RB) * bias_ref[...]) * _halves(x0_scr, first, RB)
            out = _rms(out) * g_ref[...]
            for hlf in range(2):
                out_scr[hlf, pl.ds(first, RB, stride=2), :] = out[:, hlf * LANES:(hlf + 1) * LANES]

    for c in range(nchunks):
        r = slice(c * R, (c + 1) * R)
        o_ref[0, r, :] = jnp.concatenate([out_scr[0, r, :], out_scr[1, r, :]], axis=1).astype(BF16)


def _hyena(p32, short_w, short_b, hy_bias, g_hy, mats, spec):
    B, L, _ = p32.shape
    R = min(256, L)
    M = L // 2
    col = lambda j: pl.BlockSpec((1, L, HY_CH), lambda b: (b, 0, j))
    consts = (short_w, short_b, hy_bias, g_hy) + tuple(mats) + tuple(spec)
    return pl.pallas_call(
        functools.partial(_hy_kernel, L=L, R=R),
        grid=(B,),
        in_specs=[col(1), col(2), col(3)] + [_const_spec(a.shape) for a in consts],
        out_specs=pl.BlockSpec((1, L, HY_CH), lambda b: (b, 0, 0)),
        out_shape=jax.ShapeDtypeStruct((B, L, HY_CH), BF16),
        scratch_shapes=[pltpu.VMEM((2, L, LANES), F32)] * 3 + [pltpu.VMEM((M, HY_CH), BF16)] * 4,
        compiler_params=_cparams(("parallel",)),
        name="hyena",
    )(p32, p32, p32, *consts)


def _expand_kv(kv_ref, k2_scr, vv_scr, rows):
    R = min(256, rows)
    lo = lax.broadcasted_iota(jnp.int32, (R, LANES), 1) < ATT_HD
    ones = jnp.ones((R, LANES), BF16)
    for c in range(rows // R):
        sl = slice(c * R, (c + 1) * R)
        k = kv_ref[0, sl, 0:128].astype(F32)
        v = kv_ref[0, sl, 128:256].astype(F32)
        kr = pltpu.roll(k, ATT_HD, 1)
        vr = pltpu.roll(v, ATT_HD, 1)
        k2_scr[0, sl, :] = jnp.where(lo, k, kr).astype(BF16)
        k2_scr[1, sl, :] = jnp.where(lo, kr, k).astype(BF16)
        vv_scr[0, sl, 0:128] = jnp.where(lo, v, vr).astype(BF16)
        vv_scr[1, sl, 0:128] = jnp.where(lo, vr, v).astype(BF16)
        vv_scr[0, sl, 128:256] = ones
        vv_scr[1, sl, 128:256] = ones


def _att_kernel(sink_ref, q_ref, *rest, L, Lc, band):
    if band:
        (kv_ref, ckv_ref, g_ref, o_ref, k2_scr, vv_scr, ck2_scr, cvv_scr, att_scr) = rest
        _expand_kv(kv_ref, k2_scr, vv_scr, L)
    else:
        (ckv_ref, g_ref, o_ref, ck2_scr, cvv_scr, att_scr) = rest
    _expand_kv(ckv_ref, ck2_scr, cvv_scr, Lc)

    T = ATT_BLOCK
    nb = L // T
    lane = lax.broadcasted_iota(jnp.int32, (1, LANES), 1)
    lo = lane < ATT_HD
    lo16 = jnp.where(lo, 1.0, 0.0).astype(BF16)
    hi16 = jnp.where(lo, 0.0, 1.0).astype(BF16)
    hrow = lax.broadcasted_iota(jnp.int32, (4 * T, 1), 0) >> 7
    ti = lax.broadcasted_iota(jnp.int32, (T, T), 0)
    tj = lax.broadcasted_iota(jnp.int32, (T, T), 1)
    tri_prev = jnp.where(tj >= ti, 0.0, NEG_INF)
    tri_next = jnp.where(tj <= ti, 0.0, NEG_INF)

    def add_bias(s, bias):
        return (s.reshape(4, T, T) + bias[None]).reshape(4 * T, T)

    U = min(ATT_UNROLL, nb)

    def step(it, carry):
        blocks = []
        for u in range(U):
            n = it * U + u
            blk = dict(r0=pl.multiple_of(n * T, T))
            if band:
                blk.update(rp=pl.multiple_of(jnp.maximum(n - 1, 0) * T, T),
                           rn=pl.multiple_of(jnp.minimum(n + 1, nb - 1) * T, T),
                           bias_p=tri_prev + jnp.where(n == 0, NEG_INF, 0.0),
                           bias_n=tri_next + jnp.where(n == nb - 1, NEG_INF, 0.0))
            blocks.append(blk)
        items = [(u, g) for u in range(U) for g in range(ATT_KV_HEADS)]

        def keys_of(scr, cscr, blk, g):
            parts = [cscr[g]]
            if band:
                parts += [scr[g, pl.ds(blk[k], T), :] for k in ("rp", "r0", "rn")]
            return jnp.concatenate(parts, axis=0)

        scores, sinks = [], []
        for u, g in items:
            qb = q_ref[0, pl.ds(blocks[u]["r0"], T), 256 * g:256 * (g + 1)]
            qs = []
            for p in range(2):
                qp = qb[:, 128 * p:128 * (p + 1)]
                qs += [qp * lo16, qp * hi16]
            qst = jnp.concatenate(qs, axis=0)
            scores.append(_dot_nt(qst, keys_of(k2_scr if band else None, ck2_scr, blocks[u], g)))
            sinks.append(LOG2E * jnp.where(
                hrow == 0, sink_ref[4 * g],
                jnp.where(hrow == 1, sink_ref[4 * g + 1],
                          jnp.where(hrow == 2, sink_ref[4 * g + 2], sink_ref[4 * g + 3]))))
        probs, esks = [], []
        for (u, g), s, sk in zip(items, scores, sinks):
            cols = [s[:, j * T:(j + 1) * T] for j in range(s.shape[1] // T)]
            if band:
                cols[-3] = add_bias(cols[-3], blocks[u]["bias_p"])
                cols[-1] = add_bias(cols[-1], blocks[u]["bias_n"])
            mx = cols[0]
            for c_ in cols[1:]:
                mx = jnp.maximum(mx, c_)
            m = jnp.maximum(jnp.max(mx, axis=-1, keepdims=True), sk)
            probs.append(jnp.concatenate([jnp.exp2(c_ - m).astype(BF16) for c_ in cols], axis=1))
            esks.append(jnp.exp2(sk - m))
        for (u, g), pr, esk in zip(items, probs, esks):
            pv = _dot(pr, keys_of(vv_scr if band else None, cvv_scr, blocks[u], g))
            res = pv[:, 0:LANES] / (pv[:, LANES:2 * LANES] + esk)
            for p in range(2):
                c0 = 128 * (2 * g + p)
                att_scr[u, :, c0:c0 + 128] = jnp.where(lo, res[2 * p * T:(2 * p + 1) * T],
                                                       res[(2 * p + 1) * T:(2 * p + 2) * T])
        for u in range(U):
            o_ref[0, pl.ds(blocks[u]["r0"], T), :] = (_rms(att_scr[u]) * g_ref[...]).astype(BF16)
        return carry

    for it in range(nb // U):
        step(it, 0)


def _attention(sink, p16, pc16, g_att, band):
    B, Lc, _ = pc16.shape
    L = p16.shape[1] if band else Lc
    qsrc = p16 if band else pc16
    kv_spec = lambda n: pl.BlockSpec((1, n, 256), lambda b: (b, 0, 4))
    in_specs = [pl.BlockSpec(memory_space=pltpu.SMEM),
                pl.BlockSpec((1, L, ATT_Q), lambda b: (b, 0, 1))]
    args = [sink, qsrc]
    scratch = []
    if band:
        in_specs.append(kv_spec(L))
        args.append(p16)
        scratch += [pltpu.VMEM((2, L, LANES), BF16), pltpu.VMEM((2, L, 2 * LANES), BF16)]
    in_specs += [kv_spec(Lc), _const_spec((1, ATT_Q))]
    args += [pc16, g_att]
    scratch += [pltpu.VMEM((2, Lc, LANES), BF16), pltpu.VMEM((2, Lc, 2 * LANES), BF16),
                pltpu.VMEM((ATT_UNROLL, ATT_BLOCK, ATT_Q), F32)]
    return pl.pallas_call(
        functools.partial(_att_kernel, L=L, Lc=Lc, band=band),
        grid=(B,),
        in_specs=in_specs,
        out_specs=pl.BlockSpec((1, L, ATT_Q), lambda b: (b, 0, 0)),
        out_shape=jax.ShapeDtypeStruct((B, L, ATT_Q), BF16),
        scratch_shapes=scratch,
        compiler_params=_cparams(("parallel",)),
        name="attn_window" if band else "attn_ctx",
    )(*args)


def _out_mlp_kernel(x_ref, nr_ref, nh_ref, na_ref, gta_ref, shm_ref, scm_ref, gtm_ref,
                    gpm_ref, gpre_ref, gpost_ref, wo_ref, w1_ref, w2_ref, o_ref, *, ff_chunk, sub):
    tm = x_ref.shape[1]
    dff = w1_ref.shape[1]

    def mixer_out(i):
        r = slice(i * sub, (i + 1) * sub)
        mix = (_dot(nr_ref[0, r, :], wo_ref[0:256, :]) + _dot(nh_ref[0, r, :], wo_ref[256:512, :])
               + _dot(na_ref[0, r, :], wo_ref[512:1024, :]))
        x1 = x_ref[0, r, :] + gta_ref[0] * (_rms(mix) * gpm_ref[...])
        h = ((_rms(x1) * gpre_ref[...]) * (1.0 + scm_ref[0]) + shm_ref[0]).astype(BF16)
        return x1, h

    def mlp(i, x1, h):
        acc = None
        for j in range(dff // ff_chunk):
            sl = slice(j * ff_chunk, (j + 1) * ff_chunk)
            hj = jnp.square(jnp.maximum(_dot(h, w1_ref[:, sl]), 0.0)).astype(BF16)
            part = _dot(hj, w2_ref[sl, :])
            acc = part if acc is None else acc + part
        o_ref[0, i * sub:(i + 1) * sub, :] = x1 + gtm_ref[0] * (_rms(acc) * gpost_ref[...])

    nxt = mixer_out(0)
    for i in range(tm // sub):
        cur = nxt
        if i + 1 < tm // sub:
            nxt = mixer_out(i + 1)
        mlp(i, *cur)


def _out_mlp(x3, nr, nh, na, mods, mod_row, g_post_mix, g_pre_mlp, g_post_mlp, wo16, w116, w216, l, tm):
    G, R, D = x3.shape
    row = lambda w: pl.BlockSpec((1, tm, w), lambda g, i: (g, i, 0))
    mod = lambda j: pl.BlockSpec((1, 1, D), lambda g, i: (mod_row(g), 0, j))
    return pl.pallas_call(
        functools.partial(_out_mlp_kernel, ff_chunk=1024, sub=min(512, tm)),
        grid=(G, R // tm),
        in_specs=[row(D), row(RET_V), row(HY_CH), row(ATT_Q), mod(2), mod(3), mod(4), mod(5),
                  _const_spec((1, D)), _const_spec((1, D)), _const_spec((1, D)),
                  _layer_spec(wo16, l), _layer_spec(w116, l), _layer_spec(w216, l)],
        out_specs=row(D),
        out_shape=jax.ShapeDtypeStruct((G, R, D), F32),
        compiler_params=_cparams(("parallel", "parallel")),
        name="out_mlp",
    )(x3, nr, nh, na, mods, mods, mods, mods, g_post_mix, g_pre_mlp, g_post_mlp, wo16, w116, w216)


def _rot_tables(cos, sin):
    lane = jnp.arange(LANES) % 32
    first = (lane < 16)[None, :]
    return cos, jnp.where(first, -sin, 0.0), jnp.where(first, 0.0, sin)


def _rotary_tables(L):
    pos = jnp.arange(L)
    row = (pos // GRID_W).astype(F32)
    col = (pos % GRID_W).astype(F32)
    half = ATT_HD // 2
    inv_ax = ROPE_BASE ** (-jnp.arange(0, half, 2, dtype=F32) / half)

    def tab(p, inv):
        ang = p[:, None] * inv[None, :]
        ang = jnp.concatenate([ang, ang], axis=-1)
        return jnp.cos(ang), jnp.sin(ang)

    cr, sr = tab(row, inv_ax)
    cc, sc = tab(col, inv_ax)
    inv_ret = 1.0 / (RET_ROT_BASE ** jnp.linspace(0.0, 1.0, RET_DK // 2, dtype=F32))
    ct, st = tab(pos.astype(F32), inv_ret)
    ret = _rot_tables(jnp.tile(ct, (1, 4)), jnp.tile(st, (1, 4)))
    ax = _rot_tables(jnp.tile(jnp.concatenate([cr, cc], -1), (1, 2)),
                     jnp.tile(jnp.concatenate([sr, sc], -1), (1, 2)))
    return ret + ax


def _dft_mats(L):
    n = 2 * L
    k = jnp.arange(L // 2, dtype=jnp.int32)

    def pair(t):
        ang = ((k[:, None] * t[None, :]) % n).astype(F32) * (2.0 * math.pi / n)
        return jnp.cos(ang).astype(BF16), (-jnp.sin(ang)).astype(BF16)

    ce, se = pair(2 * k)
    co, so = pair(2 * k + 1)
    return ce, se, co, so, co.T, so.T


def _filter_features(L):
    t = jnp.linspace(0.0, 1.0, L, dtype=F32)[:, None]
    w = 2.0 * math.pi * jnp.arange(L, dtype=F32) / L
    bands = jnp.linspace(1e-4, HY_BANDS - 1, HY_BANDS, dtype=F32)
    ang = w[:, None] * bands[None, :]
    z = jnp.concatenate([t, jnp.cos(ang), -jnp.sin(ang)], axis=-1)
    zf = jnp.pad(z, ((0, 0), (0, LANES - HY_EMB)))
    max_decay = math.log(HY_TARGET) / HY_FAST_DECAY
    min_decay = math.log(HY_TARGET) / HY_SLOW_DECAY
    deltas = jnp.linspace(min_decay, max_decay, HY_CH, dtype=F32)
    dec = jnp.exp(-t * jnp.abs(deltas)[None, :])
    return zf, dec


def _row_tile(rows):
    return 1024 if rows % 1024 == 0 else 512


def _pad_to(a, shape):
    return jnp.pad(a, [(0, s - d) for d, s in zip(a.shape, shape)])


def kernel(x, c, ctx, c_ctx, w_ada, b_ada, g_pre_mix, g_post_mix, g_pre_mlp, g_post_mlp, w_in,
           ret_decay_fwd, ret_decay_bwd, hy_short_w, hy_short_b, hy_f_w1, hy_f_b1, hy_f_w2, hy_f_b2,
           hy_f_w3, hy_f_freq, hy_bias, attn_sink, g_ret, g_hy, g_att, w_out, w_ff1, w_ff2):
    B, L, D = x.shape
    Lc = ctx.shape[1]
    depth = w_ada.shape[0]
    assert D == 1024 and w_in.shape[2] == 2304 and L % 256 == 0 and Lc % 256 == 0 and L >= 3 * ATT_BLOCK

    rows = -(-(B + 1) // 8) * 8
    cc = _pad_to(jnp.concatenate([c, c_ctx[None, :]], axis=0), (rows, D))
    mods_all = _ada(cc, w_ada, b_ada).reshape(depth, rows, 1, 6 * D)

    w_in16 = w_in.astype(BF16)
    w_out16 = w_out.astype(BF16)
    w_ff116 = w_ff1.astype(BF16)
    w_ff216 = w_ff2.astype(BF16)

    tabs = _rotary_tables(L)
    mats = {n: _dft_mats(n) for n in {L, Lc}}
    feats = {n: _filter_features(n) for n in {L, Lc}}

    ctx_rows = B * Lc
    tm = _row_tile(L)
    tc = _row_tile(ctx_rows)
    xc = ctx.reshape(ctx_rows // tc, tc, D)
    lat_row = lambda g: g
    ctx_row = lambda g: B

    for l in range(depth):
        last = l == depth - 1
        mods = mods_all[l]
        dec2 = jnp.stack([ret_decay_fwd[l], ret_decay_bwd[l]])
        dq = jnp.repeat(dec2, RET_DK, axis=1)
        ds = jnp.repeat(dec2, RET_CHUNK, axis=1)
        dv = jnp.repeat(dec2, RET_DV, axis=1)
        gr, gh, ga = g_ret[l][None], g_hy[l][None], g_att[l][None]
        gpre, gpm = g_pre_mix[l][None], g_post_mix[l][None]
        gprm, gpom = g_pre_mlp[l][None], g_post_mlp[l][None]
        fw1 = _pad_to(hy_f_w1[l], (LANES, LANES))
        fb1 = _pad_to(hy_f_b1[l][None], (1, LANES))
        fw2 = _pad_to(hy_f_w2[l], (2, LANES, LANES))
        fb2 = _pad_to(hy_f_b2[l][:, None, :], (2, 1, LANES))
        fw3 = _pad_to(hy_f_w3[l], (LANES, 2 * HY_CH))
        ffr = _pad_to(hy_f_freq[l][None], (1, LANES))
        sw, sb, hbias = hy_short_w[l], hy_short_b[l][None], hy_bias[l][None]

        def spectrum(n):
            zf, dec = feats[n]
            return _hyena_spectrum(n, zf, fw1, fb1, fw2, fb2, fw3, ffr, dec, mats[n])

        pc16, pc32 = _in_proj(xc, mods, ctx_row, gpre, w_in16, l, None, tc)
        pc16 = pc16.reshape(B, Lc, P16_W)
        pc32 = pc32.reshape(B, Lc, P32_W)
        cret, cstate = _retention(pc16, pc32, dq, ds, dv, gr, None, want_out=not last)

        p16, p32 = _in_proj(x, mods, lat_row, gpre, w_in16, l, tabs, tm)
        ret, _ = _retention(p16, p32, dq, ds, dv, gr, cstate)
        hyo = _hyena(p32, sw, sb, hbias, gh, mats[L], spectrum(L))
        att = _attention(attn_sink[l], p16, pc16, ga, True)
        x = _out_mlp(x, ret, hyo, att, mods, lat_row, gpm, gprm, gpom,
                     w_out16, w_ff116, w_ff216, l, tm)

        if not last:
            chyo = _hyena(pc32, sw, sb, hbias, gh, mats[Lc], spectrum(Lc))
            catt = _attention(attn_sink[l], None, pc16, ga, False)
            r3 = lambda a: a.reshape(ctx_rows // tc, tc, a.shape[-1])
            xc = _out_mlp(xc, r3(cret), r3(chyo), r3(catt), mods, ctx_row, gpm, gprm, gpom,
                          w_out16, w_ff116, w_ff216, l, tc)
    return x
```

```python
import functools
import math

import jax
import jax.numpy as jnp
from jax import lax
from jax.experimental import pallas as pl
from jax.experimental.pallas import tpu as pltpu

F32 = jnp.float32
BF16 = jnp.bfloat16
EPS = 1e-6
NEG_INF = -1e30
LOG2E = math.log2(math.e)

RET_HEADS = 4
RET_DK = 32
RET_DV = 64
RET_QK = RET_HEADS * RET_DK
RET_V = RET_HEADS * RET_DV
RET_CHUNK = 128
RET_UNROLL = 16
RET_ROT_BASE = 10000.0
HY_CH = 256
HY_EMB = 33
HY_BANDS = 16
HY_ORDER = 64
HY_FAST_DECAY = 0.3
HY_SLOW_DECAY = 1.5
HY_TARGET = 1e-2
ATT_HEADS = 8
ATT_KV_HEADS = 2
ATT_HD = 64
ATT_Q = ATT_HEADS * ATT_HD
ATT_KV = ATT_KV_HEADS * ATT_HD
WINDOW = 128
ATT_BLOCK = 128
ATT_UNROLL = 8
ROPE_BASE = 10000.0
GRID_W = 64

P16_W = 2 * RET_QK + RET_V + ATT_Q + 2 * ATT_KV
P32_W = RET_V + 3 * HY_CH

LANES = 128
VMEM_LIMIT = 56 * 1024 * 1024


def _cparams(sem):
    return pltpu.CompilerParams(dimension_semantics=sem, vmem_limit_bytes=VMEM_LIMIT)


def _const_spec(shape):
    nd = len(shape)
    return pl.BlockSpec(shape, lambda *_: (0,) * nd, pipeline_mode=pl.Buffered(1))


def _layer_spec(stacked, l):
    nd = stacked.ndim - 1
    return pl.BlockSpec((None,) + stacked.shape[1:], lambda *_: (l,) + (0,) * nd,
                        pipeline_mode=pl.Buffered(1))


def _rms(x):
    return x * lax.rsqrt(jnp.mean(x * x, axis=-1, keepdims=True) + EPS)


def _dot(a, b):
    return jnp.dot(a, b, preferred_element_type=F32)


def _dot_nt(a, b):
    return lax.dot_general(a, b, (((1,), (1,)), ((), ())), preferred_element_type=F32)


def _dot_tn(a, b):
    return lax.dot_general(a, b, (((0,), (0,)), ((), ())), preferred_element_type=F32)


def _split(a):
    hi = a.astype(BF16)
    lo = (a - hi.astype(F32)).astype(BF16)
    return hi, lo


def _dot3(a, b):
    ah, al = _split(a)
    bh, bl = _split(b)
    return _dot(ah, bh) + _dot(al, bh) + _dot(ah, bl)


def _ada_kernel(c_ref, w_ref, b_ref, o_ref):
    cv = c_ref[...]
    s = cv * jax.nn.sigmoid(cv)
    o_ref[0] = _dot(s.astype(BF16), w_ref[0].astype(BF16)) + b_ref[0]


def _ada(cc, w_ada, b_ada):
    depth, d, d6 = w_ada.shape
    rows = cc.shape[0]
    tn = 1536
    return pl.pallas_call(
        _ada_kernel,
        grid=(depth, d6 // tn),
        in_specs=[
            pl.BlockSpec((rows, d), lambda l, j: (0, 0)),
            pl.BlockSpec((1, d, tn), lambda l, j: (l, 0, j)),
            pl.BlockSpec((1, 1, tn), lambda l, j: (l, 0, j)),
        ],
        out_specs=pl.BlockSpec((1, rows, tn), lambda l, j: (l, 0, j)),
        out_shape=jax.ShapeDtypeStruct((depth, rows, d6), F32),
        compiler_params=_cparams(("arbitrary", "arbitrary")),
        name="ada",
    )(cc, w_ada, b_ada.reshape(depth, 1, d6))


def _rot(x, cos, sa, sb):
    return x * cos + pltpu.roll(x, LANES - 16, 1) * sa + pltpu.roll(x, 16, 1) * sb


def _in_kernel(x_ref, sh_ref, sc_ref, g_ref, w_ref, *rest, rotary, sub):
    if rotary:
        rc, rsa, rsb, ac, asa, asb, o16_ref, o32_ref = rest
    else:
        o16_ref, o32_ref = rest
    tm = x_ref.shape[1]
    k_scale = RET_DK ** -0.5
    q_scale = ATT_HD ** -0.5 * LOG2E

    def normed(i):
        x = x_ref[0, i * sub:(i + 1) * sub, :]
        return ((_rms(x) * g_ref[...]) * (1.0 + sc_ref[0]) + sh_ref[0]).astype(BF16)

    def project(i, hb):
        r = slice(i * sub, (i + 1) * sub)

        def rot(v, tabs):
            return _rot(v, *[t[r, :] for t in tabs]) if rotary else v

        ret_tabs = (rc, rsa, rsb) if rotary else None
        ax_tabs = (ac, asa, asb) if rotary else None
        pa = _dot(hb, w_ref[:, 1536:2304])
        for j in range(4):
            aq = rot(pa[:, 128 * j:128 * (j + 1)], ax_tabs)
            o16_ref[0, r, 512 + 128 * j:640 + 128 * j] = (aq * q_scale).astype(BF16)
        o16_ref[0, r, 1024:1152] = rot(pa[:, 512:640], ax_tabs).astype(BF16)
        o16_ref[0, r, 1152:1280] = pa[:, 640:768].astype(BF16)
        pr = _dot(hb, w_ref[:, 0:512])
        o16_ref[0, r, 0:128] = rot(pr[:, 0:128], ret_tabs).astype(BF16)
        o16_ref[0, r, 128:256] = (rot(pr[:, 128:256], ret_tabs) * k_scale).astype(BF16)
        o16_ref[0, r, 256:512] = pr[:, 256:512].astype(BF16)
        o32_ref[0, r, :] = _dot(hb, w_ref[:, 512:1536])

    nxt = normed(0)
    for i in range(tm // sub):
        cur = nxt
        if i + 1 < tm // sub:
            nxt = normed(i + 1)
        project(i, cur)


def _in_proj(x3, mods, mod_row, g_pre, w_in16, l, tabs, tm):
    G, R, D = x3.shape
    rotary = tabs is not None
    nt = R // tm
    in_specs = [
        pl.BlockSpec((1, tm, D), lambda g, i: (g, i, 0)),
        pl.BlockSpec((1, 1, D), lambda g, i: (mod_row(g), 0, 0)),
        pl.BlockSpec((1, 1, D), lambda g, i: (mod_row(g), 0, 1)),
        _const_spec((1, D)),
        _layer_spec(w_in16, l),
    ]
    args = [x3, mods, mods, g_pre, w_in16]
    if rotary:
        in_specs += [pl.BlockSpec((tm, LANES), lambda g, i: (i, 0))] * 6
        args += list(tabs)
    return pl.pallas_call(
        functools.partial(_in_kernel, rotary=rotary, sub=min(256, tm)),
        grid=(G, nt),
        in_specs=in_specs,
        out_specs=[
            pl.BlockSpec((1, tm, P16_W), lambda g, i: (g, i, 0)),
            pl.BlockSpec((1, tm, P32_W), lambda g, i: (g, i, 0)),
        ],
        out_shape=[
            jax.ShapeDtypeStruct((G, R, P16_W), BF16),
            jax.ShapeDtypeStruct((G, R, P32_W), F32),
        ],
        compiler_params=_cparams(("parallel", "parallel")),
        name="in_proj_rot" if rotary else "in_proj",
    )(*args)


def _ret_kernel(cqkv_ref, crg_ref, qkv_ref, rg_ref, dq_ref, ds_ref, dv_ref, g_ref, *rest, Lc, L, want_ctx_out):
    rest = list(rest)
    co_ref = rest.pop(0) if want_ctx_out else None
    o_ref, cst_scr, st_scr, dec_scr, dmask_scr, cdec_scr = rest
    C = RET_CHUNK

    @pl.when(pl.program_id(0) == 0)
    def _():
        def log_gamma(ref, r):
            return jnp.log1p(-jnp.exp(ref[r:r + 1, :]))

        lfq, lbq = log_gamma(dq_ref, 0), log_gamma(dq_ref, 1)
        lfs, lbs = log_gamma(ds_ref, 0), log_gamma(ds_ref, 1)
        ri = lax.broadcasted_iota(jnp.int32, (C, LANES), 0).astype(F32)
        dec_scr[0] = jnp.exp(lfq * (ri + 1.0))
        dec_scr[1] = jnp.exp(lfq * (C - 1.0 - ri))
        dec_scr[2] = jnp.exp(lbq * (C - ri))
        dec_scr[3] = jnp.exp(lbq * ri)
        di = lax.broadcasted_iota(jnp.int32, (C, 4 * C), 0)
        dj = lax.broadcasted_iota(jnp.int32, (C, 4 * C), 1) & (C - 1)
        diff = (di - dj).astype(F32)
        dmask_scr[...] = (jnp.where(diff >= 0, jnp.exp(lfs * jnp.maximum(diff, 0.0)), 0.0)
                          + jnp.where(diff <= 0, jnp.exp(lbs * jnp.maximum(-diff, 0.0)), 0.0))
        cdec_scr[0:1, :] = jnp.exp(log_gamma(dv_ref, 0) * float(C))
        cdec_scr[1:2, :] = jnp.exp(log_gamma(dv_ref, 1) * float(C))

    cdec_f, cdec_b = cdec_scr[0:1, :], cdec_scr[1:2, :]

    lane_q = lax.broadcasted_iota(jnp.int32, (1, RET_QK), 1) >> 5
    lane_v = lax.broadcasted_iota(jnp.int32, (1, RET_V), 1) >> 6
    hm = [jnp.where(lane_q == h, 1.0, 0.0).astype(BF16) for h in range(RET_HEADS)]
    cm = [jnp.where(lane_v == h, 1.0, 0.0).astype(BF16) for h in range(RET_HEADS)]
    bd = ((lax.broadcasted_iota(jnp.int32, (RET_QK, RET_V), 0) >> 5)
          == (lax.broadcasted_iota(jnp.int32, (RET_QK, RET_V), 1) >> 6))
    ones64 = jnp.where((lax.broadcasted_iota(jnp.int32, (RET_V, RET_V), 0) >> 6)
                       == (lax.broadcasted_iota(jnp.int32, (RET_V, RET_V), 1) >> 6), 1.0, 0.0).astype(BF16)

    def run(qkv_ref, rg_ref, o_ref, st_scr, L, sf0, sb0):
        N = L // C

        def kv_update(k16, v16, kdec, cdec, s):
            kd = (k16.astype(F32) * kdec).astype(BF16)
            return cdec * s + jnp.where(bd, _dot_tn(kd, v16), 0.0)

        def kv_at(n):
            r0 = pl.multiple_of(n * C, C)
            return qkv_ref[0, pl.ds(r0, C), 128:256], qkv_ref[0, pl.ds(r0, C), 256:512]

        U = min(RET_UNROLL, N)

        def scan_body(it, carry):
            sf, sb = carry
            for u in range(U):
                nf = it * U + u
                nb = N - 1 - nf
                st_scr[nf, 0:RET_QK, :] = sf.astype(BF16)
                sf = kv_update(*kv_at(nf), dec_scr[1], cdec_f, sf)
                st_scr[nb, RET_QK:2 * RET_QK, :] = sb.astype(BF16)
                sb = kv_update(*kv_at(nb), dec_scr[3], cdec_b, sb)
            return sf, sb

        sf_fin, sb_fin = lax.fori_loop(0, N // U, scan_body, (sf0, sb0))
        if o_ref is None:
            return sf_fin, sb_fin

        def out_body(it, carry):
            ns = [it * U + u for u in range(U)]
            rows = [pl.ds(pl.multiple_of(n * C, C), C) for n in ns]
            q16s = [qkv_ref[0, r, 0:128] for r in rows]
            kvs = [kv_at(n) for n in ns]
            scores = [_dot_nt(q16, jnp.concatenate([k16 * hm[h] for h in range(RET_HEADS)], axis=0))
                      for q16, (k16, _) in zip(q16s, kvs)]
            outs = []
            for n, q16, (_, v16), s in zip(ns, q16s, kvs, scores):
                qf = q16.astype(F32)
                vblk = jnp.concatenate([v16 * cm[h] for h in range(RET_HEADS)], axis=0)
                lhs = jnp.concatenate([(s * dmask_scr[...]).astype(BF16), (qf * dec_scr[0]).astype(BF16),
                                       (qf * dec_scr[2]).astype(BF16)], axis=1)
                outs.append(_dot(lhs, jnp.concatenate([vblk, st_scr[n]], axis=0)))
            mss = [_dot((o * o).astype(BF16), ones64) * (1.0 / RET_DV) for o in outs]
            for r, o, ms in zip(rows, outs, mss):
                rg = rg_ref[0, r, :]
                gated = (o * lax.rsqrt(ms + EPS)) * (rg * jax.nn.sigmoid(rg))
                o_ref[0, r, :] = (_rms(gated) * g_ref[...]).astype(BF16)
            return carry

        lax.fori_loop(0, N // U, out_body, 0)
        return sf_fin, sb_fin

    zeros = jnp.zeros((RET_QK, RET_V), F32)
    sf, sb = run(cqkv_ref, crg_ref, co_ref, cst_scr, Lc, zeros, zeros)
    run(qkv_ref, rg_ref, o_ref, st_scr, L, sf, sb)


def _retention(pc16, pc32, p16, p32, dq, ds, dv, g_ret, want_ctx_out):
    B, L, _ = p16.shape
    Lc = pc16.shape[1]
    qkv = lambda n: pl.BlockSpec((1, n, 512), lambda b: (b, 0, 0))
    tok = lambda n: pl.BlockSpec((1, n, RET_V), lambda b: (b, 0, 0))
    out_specs, out_shape = [tok(L)], [jax.ShapeDtypeStruct((B, L, RET_V), BF16)]
    if want_ctx_out:
        out_specs.insert(0, tok(Lc))
        out_shape.insert(0, jax.ShapeDtypeStruct((B, Lc, RET_V), BF16))
    res = pl.pallas_call(
        functools.partial(_ret_kernel, Lc=Lc, L=L, want_ctx_out=want_ctx_out),
        grid=(B,),
        in_specs=[qkv(Lc), tok(Lc), qkv(L), tok(L), _const_spec(dq.shape), _const_spec(ds.shape),
                  _const_spec(dv.shape), _const_spec((1, RET_V))],
        out_specs=out_specs,
        out_shape=out_shape,
        scratch_shapes=[pltpu.VMEM((Lc // RET_CHUNK, 2 * RET_QK, RET_V), BF16),
                        pltpu.VMEM((L // RET_CHUNK, 2 * RET_QK, RET_V), BF16),
                        pltpu.VMEM((4, RET_CHUNK, LANES), F32),
                        pltpu.VMEM((RET_CHUNK, 4 * RET_CHUNK), F32),
                        pltpu.VMEM((2, RET_V), F32)],
        compiler_params=_cparams(("arbitrary",)),
        name="retention",
    )(pc16, pc32, p16, p32, dq, ds, dv, g_ret)
    return tuple(res) if want_ctx_out else (None, res[0])


def _halves(scr, start, rows):
    return jnp.concatenate([scr[0, pl.ds(start, rows, stride=2), :],
                            scr[1, pl.ds(start, rows, stride=2), :]], axis=1)


def _alt_sign(rows, first=0):
    r = lax.broadcasted_iota(jnp.int32, (rows, HY_CH), 0) + first
    return jnp.where((r & 1) == 1, -1.0, 1.0)


def _filt_kernel(zf_ref, w1_ref, b1_ref, w2_ref, b2_ref, w3_ref, fr_ref, dec_ref,
                 ce_ref, se_ref, co_ref, so_ref, hra_ref, hia_ref, hrb_ref, hib_ref, hm_ref,
                 a_scr, d_scr, *, L):
    n = 2 * L
    M = L // 2
    fr = fr_ref[...]
    h = jnp.sin(fr * (_dot3(zf_ref[...], w1_ref[...]) + b1_ref[...]))
    for j in range(2):
        h = jnp.sin(fr * (_dot3(h, w2_ref[j]) + b2_ref[j]))
    h = _dot3(h, w3_ref[...])
    dec = dec_ref[...]
    row = lax.broadcasted_iota(jnp.int32, (L, HY_CH), 0)
    hf = h[:, 0:HY_CH] * dec
    hb = jnp.where(row == 0, 0.0, h[:, HY_CH:2 * HY_CH] * dec)
    for scr, val in ((a_scr, hf + hb), (d_scr, hf - hb)):
        scr[0] = val[:, 0:LANES]
        scr[1] = val[:, LANES:2 * LANES]
    ae, ao = _halves(a_scr, 0, M), _halves(a_scr, 1, M)
    de, do = _halves(d_scr, 0, M), _halves(d_scr, 1, M)
    sgn = _alt_sign(M)
    hm_ref[0:1, :] = jnp.sum(ae * sgn, axis=0, keepdims=True) * (2.0 / n)
    hm_ref[1:2, :] = jnp.sum(do * sgn, axis=0, keepdims=True) * (-2.0 / n)

    def dft(m_ref, x):
        xh, xl = _split(x)
        return _dot(m_ref[...], xh) + _dot(m_ref[...], xl)

    wk = jnp.where(lax.broadcasted_iota(jnp.int32, (M, 1), 0) == 0, 1.0 / n, 2.0 / n)
    ea, ta = dft(ce_ref, ae), dft(co_ref, ao)
    hra_ref[...] = (ea + ta) * wk
    hrb_ref[...] = (ea - ta) * wk
    ed, td = dft(se_ref, de), dft(so_ref, do)
    hia_ref[...] = (ed + td) * wk
    hib_ref[...] = (td - ed) * wk


def _hyena_spectrum(L, zf, w1, b1, w2, b2, w3, fr, dec, mats):
    M = L // 2
    args = (zf, w1, b1, w2, b2, w3, fr, dec) + tuple(mats[:4])
    half = jax.ShapeDtypeStruct((M, HY_CH), F32)
    return pl.pallas_call(
        functools.partial(_filt_kernel, L=L),
        grid=(1,),
        in_specs=[_const_spec(a.shape) for a in args],
        out_specs=[_const_spec((M, HY_CH))] * 4 + [_const_spec((2, HY_CH))],
        out_shape=[half] * 4 + [jax.ShapeDtypeStruct((2, HY_CH), F32)],
        scratch_shapes=[pltpu.VMEM((2, L, LANES), F32)] * 2,
        compiler_params=_cparams(("arbitrary",)),
        name="hyena_filter",
    )(*args)


def _hy_kernel(v_ref, x1_ref, x0_ref, sw_ref, sb_ref, bias_ref, g_ref,
               ce_ref, se_ref, co_ref, so_ref, cot_ref, sot_ref,
               hra_ref, hia_ref, hrb_ref, hib_ref, hm_ref, o_ref,
               z_scr, x0_scr, out_scr, ger_scr, gei_scr, gor_scr, goi_scr, *, L, R):
    nchunks = L // R
    M = L // 2
    RB = min(256, M)
    for c in range(nchunks):
        r0 = c * R
        lo = max(r0 - 8, 0)
        hi = min(r0 + R + 8, L)
        rows = hi - lo
        off = r0 - lo
        grow = lax.broadcasted_iota(jnp.int32, (R, HY_CH), 0) + r0

        def conv(ref, c0):
            ext = ref[0, lo:hi, :]
            up = pltpu.roll(ext, 1, 0)[off:off + R]
            un = pltpu.roll(ext, rows - 1, 0)[off:off + R]
            if c == 0:
                up = jnp.where(grow == 0, 0.0, up)
            if c == nchunks - 1:
                un = jnp.where(grow == L - 1, 0.0, un)
            u = ref[0, r0:r0 + R, :]
            return (up * sw_ref[0:1, c0:c0 + HY_CH] + u * sw_ref[1:2, c0:c0 + HY_CH]
                    + un * sw_ref[2:3, c0:c0 + HY_CH] + sb_ref[:, c0:c0 + HY_CH])

        z = conv(v_ref, 0) * conv(x1_ref, HY_CH)
        x0 = conv(x0_ref, 2 * HY_CH)
        for hlf in range(2):
            z_scr[hlf, r0:r0 + R, :] = z[:, hlf * LANES:(hlf + 1) * LANES]
            x0_scr[hlf, r0:r0 + R, :] = x0[:, hlf * LANES:(hlf + 1) * LANES]

    ze, zo = _halves(z_scr, 0, M), _halves(z_scr, 1, M)
    sgn = _alt_sign(M)
    mid_r = jnp.sum(ze * sgn, axis=0, keepdims=True)
    mid_i = -jnp.sum(zo * sgn, axis=0, keepdims=True)
    ym_r = mid_r * hm_ref[0:1, :] - mid_i * hm_ref[1:2, :]
    ym_i = mid_r * hm_ref[1:2, :] + mid_i * hm_ref[0:1, :]
    zeb, zob = ze.astype(BF16), zo.astype(BF16)
    for kb in range(M // RB):
        ks = slice(kb * RB, (kb + 1) * RB)
        er, ei = _dot(ce_ref[ks, :], zeb), _dot(se_ref[ks, :], zeb)
        tr, ti = _dot(co_ref[ks, :], zob), _dot(so_ref[ks, :], zob)
        ar, ai, br, bi = er + tr, ei + ti, er - tr, ti - ei
        hra, hia, hrb, hib = hra_ref[ks, :], hia_ref[ks, :], hrb_ref[ks, :], hib_ref[ks, :]
        yar, yai = ar * hra - ai * hia, ar * hia + ai * hra
        ybr, ybi = br * hrb - bi * hib, br * hib + bi * hrb
        ger_scr[ks, :] = (yar + ybr).astype(BF16)
        gei_scr[ks, :] = (yai - ybi).astype(BF16)
        gor_scr[ks, :] = (yar - ybr).astype(BF16)
        goi_scr[ks, :] = (yai + ybi).astype(BF16)

    ger, gei, gor, goi = ger_scr[...], gei_scr[...], gor_scr[...], goi_scr[...]
    for tb in range(M // RB):
        ts = slice(tb * RB, (tb + 1) * RB)
        sg = _alt_sign(RB, tb * RB)
        ys = (_dot(ce_ref[ts, :], ger) + _dot(se_ref[ts, :], gei) + sg * ym_r,
              _dot(cot_ref[ts, :], gor) + _dot(sot_ref[ts, :], goi) - sg * ym_i)
        for par, y in enumerate(ys):
            first = 2 * tb * RB + par
            out = (y + _halves(z_scr, first, RB) * bias_ref[...]) * _halves(x0_scr, first, RB)
            out = _rms(out) * g_ref[...]
            for hlf in range(2):
                out_scr[hlf, pl.ds(first, RB, stride=2), :] = out[:, hlf * LANES:(hlf + 1) * LANES]

    for c in range(nchunks):
        r = slice(c * R, (c + 1) * R)
        o_ref[0, r, :] = jnp.concatenate([out_scr[0, r, :], out_scr[1, r, :]], axis=1).astype(BF16)


def _hyena(p32, short_w, short_b, hy_bias, g_hy, mats, spec):
    B, L, _ = p32.shape
    R = min(256, L)
    M = L // 2
    col = lambda j: pl.BlockSpec((1, L, HY_CH), lambda b: (b, 0, j))
    consts = (short_w, short_b, hy_bias, g_hy) + tuple(mats) + tuple(spec)
    return pl.pallas_call(
        functools.partial(_hy_kernel, L=L, R=R),
        grid=(B,),
        in_specs=[col(1), col(2), col(3)] + [_const_spec(a.shape) for a in consts],
        out_specs=pl.BlockSpec((1, L, HY_CH), lambda b: (b, 0, 0)),
        out_shape=jax.ShapeDtypeStruct((B, L, HY_CH), BF16),
        scratch_shapes=[pltpu.VMEM((2, L, LANES), F32)] * 3 + [pltpu.VMEM((M, HY_CH), BF16)] * 4,
        compiler_params=_cparams(("parallel",)),
        name="hyena",
    )(p32, p32, p32, *consts)


def _expand_kv(kv_ref, k2_scr, vv_scr, rows):
    R = min(256, rows)
    lo = lax.broadcasted_iota(jnp.int32, (R, LANES), 1) < ATT_HD
    ones = jnp.ones((R, LANES), BF16)
    for c in range(rows // R):
        sl = slice(c * R, (c + 1) * R)
        k = kv_ref[0, sl, 0:128].astype(F32)
        v = kv_ref[0, sl, 128:256].astype(F32)
        kr = pltpu.roll(k, ATT_HD, 1)
        vr = pltpu.roll(v, ATT_HD, 1)
        k2_scr[0, sl, :] = jnp.where(lo, k, kr).astype(BF16)
        k2_scr[1, sl, :] = jnp.where(lo, kr, k).astype(BF16)
        vv_scr[0, sl, 0:128] = jnp.where(lo, v, vr).astype(BF16)
        vv_scr[1, sl, 0:128] = jnp.where(lo, vr, v).astype(BF16)
        vv_scr[0, sl, 128:256] = ones
        vv_scr[1, sl, 128:256] = ones


def _att_kernel(sink_ref, q_ref, *rest, L, Lc, band):
    if band:
        (kv_ref, ckv_ref, g_ref, o_ref, k2_scr, vv_scr, ck2_scr, cvv_scr, att_scr) = rest
        _expand_kv(kv_ref, k2_scr, vv_scr, L)
    else:
        (ckv_ref, g_ref, o_ref, ck2_scr, cvv_scr, att_scr) = rest
    _expand_kv(ckv_ref, ck2_scr, cvv_scr, Lc)

    T = ATT_BLOCK
    nb = L // T
    lane = lax.broadcasted_iota(jnp.int32, (1, LANES), 1)
    lo = lane < ATT_HD
    lo16 = jnp.where(lo, 1.0, 0.0).astype(BF16)
    hi16 = jnp.where(lo, 0.0, 1.0).astype(BF16)
    hrow = lax.broadcasted_iota(jnp.int32, (4 * T, 1), 0) >> 7
    ti = lax.broadcasted_iota(jnp.int32, (T, T), 0)
    tj = lax.broadcasted_iota(jnp.int32, (T, T), 1)
    tri_prev = jnp.where(tj >= ti, 0.0, NEG_INF)
    tri_next = jnp.where(tj <= ti, 0.0, NEG_INF)

    def add_bias(s, bias):
        return (s.reshape(4, T, T) + bias[None]).reshape(4 * T, T)

    U = min(ATT_UNROLL, nb)

    def step(it, carry):
        blocks = []
        for u in range(U):
            n = it * U + u
            blk = dict(r0=pl.multiple_of(n * T, T))
            if band:
                blk.update(rp=pl.multiple_of(jnp.maximum(n - 1, 0) * T, T),
                           rn=pl.multiple_of(jnp.minimum(n + 1, nb - 1) * T, T),
                           bias_p=tri_prev + jnp.where(n == 0, NEG_INF, 0.0),
                           bias_n=tri_next + jnp.where(n == nb - 1, NEG_INF, 0.0))
            blocks.append(blk)
        items = [(u, g) for u in range(U) for g in range(ATT_KV_HEADS)]

        def keys_of(scr, cscr, blk, g):
            parts = [cscr[g]]
            if band:
                parts += [scr[g, pl.ds(blk[k], T), :] for k in ("rp", "r0", "rn")]
            return jnp.concatenate(parts, axis=0)

        scores, sinks = [], []
        for u, g in items:
            qb = q_ref[0, pl.ds(blocks[u]["r0"], T), 256 * g:256 * (g + 1)]
            qs = []
            for p in range(2):
                qp = qb[:, 128 * p:128 * (p + 1)]
                qs += [qp * lo16, qp * hi16]
            qst = jnp.concatenate(qs, axis=0)
            scores.append(_dot_nt(qst, keys_of(k2_scr if band else None, ck2_scr, blocks[u], g)))
            sinks.append(LOG2E * jnp.where(
                hrow == 0, sink_ref[4 * g],
                jnp.where(hrow == 1, sink_ref[4 * g + 1],
                          jnp.where(hrow == 2, sink_ref[4 * g + 2], sink_ref[4 * g + 3]))))
        probs, esks = [], []
        for (u, g), s, sk in zip(items, scores, sinks):
            cols = [s[:, j * T:(j + 1) * T] for j in range(s.shape[1] // T)]
            if band:
                cols[-3] = add_bias(cols[-3], blocks[u]["bias_p"])
                cols[-1] = add_bias(cols[-1], blocks[u]["bias_n"])
            mx = cols[0]
            for c_ in cols[1:]:
                mx = jnp.maximum(mx, c_)
            m = jnp.maximum(jnp.max(mx, axis=-1, keepdims=True), sk)
            probs.append(jnp.concatenate([jnp.exp2(c_ - m).astype(BF16) for c_ in cols], axis=1))
            esks.append(jnp.exp2(sk - m))
        for (u, g), pr, esk in zip(items, probs, esks):
            pv = _dot(pr, keys_of(vv_scr if band else None, cvv_scr, blocks[u], g))
            res = pv[:, 0:LANES] / (pv[:, LANES:2 * LANES] + esk)
            for p in range(2):
                c0 = 128 * (2 * g + p)
                att_scr[u, :, c0:c0 + 128] = jnp.where(lo, res[2 * p * T:(2 * p + 1) * T],
                                                       res[(2 * p + 1) * T:(2 * p + 2) * T])
        for u in range(U):
            o_ref[0, pl.ds(blocks[u]["r0"], T), :] = (_rms(att_scr[u]) * g_ref[...]).astype(BF16)
        return carry

    lax.fori_loop(0, nb // U, step, 0)


def _attention(sink, p16, pc16, g_att, band):
    B, Lc, _ = pc16.shape
    L = p16.shape[1] if band else Lc
    qsrc = p16 if band else pc16
    kv_spec = lambda n: pl.BlockSpec((1, n, 256), lambda b: (b, 0, 4))
    in_specs = [pl.BlockSpec(memory_space=pltpu.SMEM),
                pl.BlockSpec((1, L, ATT_Q), lambda b: (b, 0, 1))]
    args = [sink, qsrc]
    scratch = []
    if band:
        in_specs.append(kv_spec(L))
        args.append(p16)
        scratch += [pltpu.VMEM((2, L, LANES), BF16), pltpu.VMEM((2, L, 2 * LANES), BF16)]
    in_specs += [kv_spec(Lc), _const_spec((1, ATT_Q))]
    args += [pc16, g_att]
    scratch += [pltpu.VMEM((2, Lc, LANES), BF16), pltpu.VMEM((2, Lc, 2 * LANES), BF16),
                pltpu.VMEM((ATT_UNROLL, ATT_BLOCK, ATT_Q), F32)]
    return pl.pallas_call(
        functools.partial(_att_kernel, L=L, Lc=Lc, band=band),
        grid=(B,),
        in_specs=in_specs,
        out_specs=pl.BlockSpec((1, L, ATT_Q), lambda b: (b, 0, 0)),
        out_shape=jax.ShapeDtypeStruct((B, L, ATT_Q), BF16),
        scratch_shapes=scratch,
        compiler_params=_cparams(("parallel",)),
        name="attn_window" if band else "attn_ctx",
    )(*args)


def _out_mlp_kernel(x_ref, nr_ref, nh_ref, na_ref, gta_ref, shm_ref, scm_ref, gtm_ref,
                    gpm_ref, gpre_ref, gpost_ref, wo_ref, w1_ref, w2_ref, o_ref, *, ff_chunk, sub):
    tm = x_ref.shape[1]
    dff = w1_ref.shape[1]

    def mixer_out(i):
        r = slice(i * sub, (i + 1) * sub)
        mix = (_dot(nr_ref[0, r, :], wo_ref[0:256, :]) + _dot(nh_ref[0, r, :], wo_ref[256:512, :])
               + _dot(na_ref[0, r, :], wo_ref[512:1024, :]))
        x1 = x_ref[0, r, :] + gta_ref[0] * (_rms(mix) * gpm_ref[...])
        h = ((_rms(x1) * gpre_ref[...]) * (1.0 + scm_ref[0]) + shm_ref[0]).astype(BF16)
        return x1, h

    def mlp(i, x1, h):
        acc = None
        for j in range(dff // ff_chunk):
            sl = slice(j * ff_chunk, (j + 1) * ff_chunk)
            hj = jnp.square(jnp.maximum(_dot(h, w1_ref[:, sl]), 0.0)).astype(BF16)
            part = _dot(hj, w2_ref[sl, :])
            acc = part if acc is None else acc + part
        o_ref[0, i * sub:(i + 1) * sub, :] = x1 + gtm_ref[0] * (_rms(acc) * gpost_ref[...])

    nxt = mixer_out(0)
    for i in range(tm // sub):
        cur = nxt
        if i + 1 < tm // sub:
            nxt = mixer_out(i + 1)
        mlp(i, *cur)


def _out_mlp(x3, nr, nh, na, mods, mod_row, g_post_mix, g_pre_mlp, g_post_mlp, wo16, w116, w216, l, tm):
    G, R, D = x3.shape
    row = lambda w: pl.BlockSpec((1, tm, w), lambda g, i: (g, i, 0))
    mod = lambda j: pl.BlockSpec((1, 1, D), lambda g, i: (mod_row(g), 0, j))
    return pl.pallas_call(
        functools.partial(_out_mlp_kernel, ff_chunk=1024, sub=min(512, tm)),
        grid=(G, R // tm),
        in_specs=[row(D), row(RET_V), row(HY_CH), row(ATT_Q), mod(2), mod(3), mod(4), mod(5),
                  _const_spec((1, D)), _const_spec((1, D)), _const_spec((1, D)),
                  _layer_spec(wo16, l), _layer_spec(w116, l), _layer_spec(w216, l)],
        out_specs=row(D),
        out_shape=jax.ShapeDtypeStruct((G, R, D), F32),
        compiler_params=_cparams(("parallel", "parallel")),
        name="out_mlp",
    )(x3, nr, nh, na, mods, mods, mods, mods, g_post_mix, g_pre_mlp, g_post_mlp, wo16, w116, w216)


def _rot_tables(cos, sin):
    lane = jnp.arange(LANES) % 32
    first = (lane < 16)[None, :]
    return cos, jnp.where(first, -sin, 0.0), jnp.where(first, 0.0, sin)


def _rotary_tables(L):
    pos = jnp.arange(L)
    row = (pos // GRID_W).astype(F32)
    col = (pos % GRID_W).astype(F32)
    half = ATT_HD // 2
    inv_ax = ROPE_BASE ** (-jnp.arange(0, half, 2, dtype=F32) / half)

    def tab(p, inv):
        ang = p[:, None] * inv[None, :]
        ang = jnp.concatenate([ang, ang], axis=-1)
        return jnp.cos(ang), jnp.sin(ang)

    cr, sr = tab(row, inv_ax)
    cc, sc = tab(col, inv_ax)
    inv_ret = 1.0 / (RET_ROT_BASE ** jnp.linspace(0.0, 1.0, RET_DK // 2, dtype=F32))
    ct, st = tab(pos.astype(F32), inv_ret)
    ret = _rot_tables(jnp.tile(ct, (1, 4)), jnp.tile(st, (1, 4)))
    ax = _rot_tables(jnp.tile(jnp.concatenate([cr, cc], -1), (1, 2)),
                     jnp.tile(jnp.concatenate([sr, sc], -1), (1, 2)))
    return ret + ax


def _dft_mats(L):
    n = 2 * L
    k = jnp.arange(L // 2, dtype=jnp.int32)

    def pair(t):
        ang = ((k[:, None] * t[None, :]) % n).astype(F32) * (2.0 * math.pi / n)
        return jnp.cos(ang).astype(BF16), (-jnp.sin(ang)).astype(BF16)

    ce, se = pair(2 * k)
    co, so = pair(2 * k + 1)
    return ce, se, co, so, co.T, so.T


def _filter_features(L):
    t = jnp.linspace(0.0, 1.0, L, dtype=F32)[:, None]
    w = 2.0 * math.pi * jnp.arange(L, dtype=F32) / L
    bands = jnp.linspace(1e-4, HY_BANDS - 1, HY_BANDS, dtype=F32)
    ang = w[:, None] * bands[None, :]
    z = jnp.concatenate([t, jnp.cos(ang), -jnp.sin(ang)], axis=-1)
    zf = jnp.pad(z, ((0, 0), (0, LANES - HY_EMB)))
    max_decay = math.log(HY_TARGET) / HY_FAST_DECAY
    min_decay = math.log(HY_TARGET) / HY_SLOW_DECAY
    deltas = jnp.linspace(min_decay, max_decay, HY_CH, dtype=F32)
    dec = jnp.exp(-t * jnp.abs(deltas)[None, :])
    return zf, dec


def _row_tile(rows):
    return 1024 if rows % 1024 == 0 else 512


def _pad_to(a, shape):
    return jnp.pad(a, [(0, s - d) for d, s in zip(a.shape, shape)])


def kernel(x, c, ctx, c_ctx, w_ada, b_ada, g_pre_mix, g_post_mix, g_pre_mlp, g_post_mlp, w_in,
           ret_decay_fwd, ret_decay_bwd, hy_short_w, hy_short_b, hy_f_w1, hy_f_b1, hy_f_w2, hy_f_b2,
           hy_f_w3, hy_f_freq, hy_bias, attn_sink, g_ret, g_hy, g_att, w_out, w_ff1, w_ff2):
    B, L, D = x.shape
    Lc = ctx.shape[1]
    depth = w_ada.shape[0]
    assert D == 1024 and w_in.shape[2] == 2304 and L % 256 == 0 and Lc % 256 == 0 and L >= 3 * ATT_BLOCK

    rows = -(-(B + 1) // 8) * 8
    cc = _pad_to(jnp.concatenate([c, c_ctx[None, :]], axis=0), (rows, D))
    mods_all = _ada(cc, w_ada, b_ada).reshape(depth, rows, 1, 6 * D)

    w_in16 = w_in.astype(BF16)
    w_out16 = w_out.astype(BF16)
    w_ff116 = w_ff1.astype(BF16)
    w_ff216 = w_ff2.astype(BF16)

    tabs = _rotary_tables(L)
    mats = {n: _dft_mats(n) for n in {L, Lc}}
    feats = {n: _filter_features(n) for n in {L, Lc}}

    ctx_rows = B * Lc
    tm = _row_tile(L)
    tc = _row_tile(ctx_rows)
    xc = ctx.reshape(ctx_rows // tc, tc, D)
    lat_row = lambda g: g
    ctx_row = lambda g: B

    for l in range(depth):
        last = l == depth - 1
        mods = mods_all[l]
        dec2 = jnp.stack([ret_decay_fwd[l], ret_decay_bwd[l]])
        dq = jnp.repeat(dec2, RET_DK, axis=1)
        ds = jnp.repeat(dec2, RET_CHUNK, axis=1)
        dv = jnp.repeat(dec2, RET_DV, axis=1)
        gr, gh, ga = g_ret[l][None], g_hy[l][None], g_att[l][None]
        gpre, gpm = g_pre_mix[l][None], g_post_mix[l][None]
        gprm, gpom = g_pre_mlp[l][None], g_post_mlp[l][None]
        fw1 = _pad_to(hy_f_w1[l], (LANES, LANES))
        fb1 = _pad_to(hy_f_b1[l][None], (1, LANES))
        fw2 = _pad_to(hy_f_w2[l], (2, LANES, LANES))
        fb2 = _pad_to(hy_f_b2[l][:, None, :], (2, 1, LANES))
        fw3 = _pad_to(hy_f_w3[l], (LANES, 2 * HY_CH))
        ffr = _pad_to(hy_f_freq[l][None], (1, LANES))
        sw, sb, hbias = hy_short_w[l], hy_short_b[l][None], hy_bias[l][None]

        def spectrum(n):
            zf, dec = feats[n]
            return _hyena_spectrum(n, zf, fw1, fb1, fw2, fb2, fw3, ffr, dec, mats[n])

        pc16, pc32 = _in_proj(xc, mods, ctx_row, gpre, w_in16, l, None, tc)
        pc16 = pc16.reshape(B, Lc, P16_W)
        pc32 = pc32.reshape(B, Lc, P32_W)

        p16, p32 = _in_proj(x, mods, lat_row, gpre, w_in16, l, tabs, tm)
        cret, ret = _retention(pc16, pc32, p16, p32, dq, ds, dv, gr, not last)
        hyo = _hyena(p32, sw, sb, hbias, gh, mats[L], spectrum(L))
        att = _attention(attn_sink[l], p16, pc16, ga, True)
        x = _out_mlp(x, ret, hyo, att, mods, lat_row, gpm, gprm, gpom,
                     w_out16, w_ff116, w_ff216, l, tm)

        if not last:
            chyo = _hyena(pc32, sw, sb, hbias, gh, mats[Lc], spectrum(Lc))
            catt = _attention(attn_sink[l], None, pc16, ga, False)
            r3 = lambda a: a.reshape(ctx_rows // tc, tc, a.shape[-1])
            xc = _out_mlp(xc, r3(cret), r3(chyo), r3(catt), mods, ctx_row, gpm, gprm, gpom,
                          w_out16, w_ff116, w_ff216, l, tc)
    return x
```
